```python
import math
import jax, jax.numpy as jnp
from jax import lax
import numpy as np

D_MODEL = 1024
BATCH = 32
SEQ = 2048
DEPTH = 4

N_MEM = 256
N_MIXERS = 2
N_A = (DEPTH + 1) // 2
N_B = DEPTH // 2
MIX_WIDTH = D_MODEL
MEM_HEADS = 4
MEM_HEAD_DIM = MIX_WIDTH // 4 // MEM_HEADS
MEM_WIDTH = MEM_HEADS * MEM_HEAD_DIM
MIXER_WIDTH = MIX_WIDTH - MEM_WIDTH

GLA_HEADS = 4
GLA_DV = MIXER_WIDTH // GLA_HEADS
GLA_DK = GLA_DV // 2
GLA_HK = GLA_HEADS * GLA_DK
GLA_GATE_RANK = 16
GLA_GATE_TAU = 16.0
GLA_CHUNK = 64
GLA_IN = 2 * GLA_HK + MIXER_WIDTH + GLA_GATE_RANK + MEM_WIDTH + MIX_WIDTH

DIL_GROUPS = ((128, 1), (512, 4), (2048, 16))
DIL_HEADS = 6
DIL_HEAD_DIM = MIXER_WIDTH // DIL_HEADS
DIL_IN = len(DIL_GROUPS) * 3 * MIXER_WIDTH + MEM_WIDTH + MIX_WIDTH

ROPE_THETA = 500000.0
ROPE_DIM = DIL_HEAD_DIM // 4
NORM_EPS = 1e-6

kernel_name = "hybrid_gla_dilated_memxattn"


def rmsnorm(x, w):
    xf = x.astype(jnp.float32)
    y = xf * lax.rsqrt(jnp.mean(xf * xf, axis=-1, keepdims=True) + NORM_EPS)
    return (y * w.astype(jnp.float32)).astype(x.dtype)


def partial_rope(x, pos):
    half = ROPE_DIM // 2
    inv = ROPE_THETA ** (-jnp.arange(half, dtype=jnp.float32) / half)
    ang = pos.astype(jnp.float32)[:, None] * inv[None, :]
    cos = jnp.cos(ang)[None, :, None, :]
    sin = jnp.sin(ang)[None, :, None, :]
    xr = x[..., :ROPE_DIM].astype(jnp.float32)
    x1, x2 = xr[..., :half], xr[..., half:]
    rot = jnp.concatenate([x1 * cos - x2 * sin, x1 * sin + x2 * cos], axis=-1).astype(x.dtype)
    return jnp.concatenate([rot, x[..., ROPE_DIM:]], axis=-1)


def memory_attention(q, mem_n, w_memkv):
    B, T, _ = q.shape
    kv = mem_n @ w_memkv
    k, v = jnp.split(kv, 2, axis=-1)
    q = q.reshape(B, T, MEM_HEADS, MEM_HEAD_DIM)
    k = k.reshape(B, N_MEM, MEM_HEADS, MEM_HEAD_DIM)
    v = v.reshape(B, N_MEM, MEM_HEADS, MEM_HEAD_DIM)
    s = jnp.einsum('bthd,bnhd->bhtn', q, k).astype(jnp.float32) * (MEM_HEAD_DIM ** -0.5)
    p = jax.nn.softmax(s, axis=-1).astype(v.dtype)
    o = jnp.einsum('bhtn,bnhd->bthd', p, v)
    return o.reshape(B, T, MEM_WIDTH)


def gla_chunked(q, k, v, log_a):
    B, T, H, Dk = q.shape
    Dv = v.shape[-1]
    C = GLA_CHUNK
    N = T // C
    f32 = jnp.float32
    q = (q.astype(f32) * (Dk ** -0.5)).reshape(B, N, C, H, Dk)
    k = k.astype(f32).reshape(B, N, C, H, Dk)
    v = v.astype(f32).reshape(B, N, C, H, Dv)
    b = jnp.cumsum(log_a.astype(f32).reshape(B, N, C, H, Dk), axis=2)
    b_last = b[:, :, -1:]
    q_in = q * jnp.exp(b)
    k_in = k * jnp.exp(-b)
    k_out = k * jnp.exp(b_last - b)
    A = jnp.einsum('bnihd,bnjhd->bnhij', q_in, k_in)
    causal = jnp.tril(jnp.ones((C, C), dtype=bool))
    A = jnp.where(causal, A, 0.0)
    o_intra = jnp.einsum('bnhij,bnjhe->bnihe', A, v)
    dS = jnp.einsum('bnjhd,bnjhe->nbhde', k_out, v)
    decay = jnp.exp(b_last[:, :, 0]).transpose(1, 0, 2, 3)

    def step(S, inp):
        dS_c, dec_c = inp
        return dec_c[..., None] * S + dS_c, S

    S0 = jnp.zeros((B, H, Dk, Dv), f32)
    _, S_prev = lax.scan(step, S0, (dS, decay))
    o_inter = jnp.einsum('bnihd,nbhde->bnihe', q_in, S_prev)
    return (o_intra + o_inter).reshape(B, T, H, Dv)


def gla_mixer(h, w_in, w_gate_up, b_gate, gla_norm_w):
    B, T, _ = h.shape
    proj = h @ w_in
    cuts = np.cumsum([GLA_HK, GLA_HK, MIXER_WIDTH, GLA_GATE_RANK, MEM_WIDTH]).tolist()
    q, k, v, g_low, q_mem, gate = jnp.split(proj, cuts, axis=-1)
    log_a = jax.nn.log_sigmoid((g_low @ w_gate_up + b_gate).astype(jnp.float32)) / GLA_GATE_TAU
    o = gla_chunked(q.reshape(B, T, GLA_HEADS, GLA_DK),
                    k.reshape(B, T, GLA_HEADS, GLA_DK),
                    v.reshape(B, T, GLA_HEADS, GLA_DV),
                    log_a.reshape(B, T, GLA_HEADS, GLA_DK))
    o = rmsnorm(o, gla_norm_w).astype(h.dtype).reshape(B, T, MIXER_WIDTH)
    return o, q_mem, gate


def dilated_group(q, k, v, window, dilation):
    B, T, H, Dh = q.shape
    r = dilation
    n = window // dilation
    L = T // r
    nb = -(-L // n)
    Lp = nb * n

    def phase_blocks(x):
        x = x.reshape(B, L, r, H, Dh).transpose(0, 2, 3, 1, 4)
        x = jnp.pad(x, ((0, 0), (0, 0), (0, 0), (0, Lp - L), (0, 0)))
        return x.reshape(B, r, H, nb, n, Dh)

    def with_prev(x):
        prev = jnp.pad(x, ((0, 0), (0, 0), (0, 0), (1, 0), (0, 0), (0, 0)))[:, :, :, :-1]
        return jnp.concatenate([prev, x], axis=4)

    qb = phase_blocks(q)
    kw = with_prev(phase_blocks(k))
    vw = with_prev(phase_blocks(v))
    s = jnp.einsum('bphjqd,bphjkd->bphjqk', qb, kw).astype(jnp.float32) * (Dh ** -0.5)
    qi = jnp.arange(n)[:, None] + n
    ki = jnp.arange(2 * n)[None, :]
    dist = qi - ki
    key_abs = jnp.arange(nb)[:, None, None] * n + ki[None] - n
    mask = (dist >= 0)[None] & (dist <= n)[None] & (key_abs >= 0)
    s = jnp.where(mask, s, -jnp.inf)
    m = jnp.max(s, axis=-1, keepdims=True)
    p = jnp.exp(s - m)
    den = jnp.sum(p, axis=-1, keepdims=True)
    o = jnp.einsum('bphjqk,bphjkd->bphjqd', (p / den).astype(v.dtype), vw)
    lse = (m + jnp.log(den))[..., 0]
    o = o.reshape(B, r, H, Lp, Dh)[:, :, :, :L].transpose(0, 3, 1, 2, 4).reshape(B, T, H, Dh)
    lse = lse.reshape(B, r, H, Lp)[..., :L].transpose(0, 3, 1, 2).reshape(B, T, H)
    return o, lse


def dilated_mixer(h, w_in):
    B, T, _ = h.shape
    pos = jnp.arange(T)
    outs, lses = [], []
    for g, (window, dil) in enumerate(DIL_GROUPS):
        base = g * 3 * MIXER_WIDTH
        qkv = (h @ w_in[:, base:base + 3 * MIXER_WIDTH]).reshape(B, T, 3, DIL_HEADS, DIL_HEAD_DIM)
        q = partial_rope(qkv[:, :, 0], pos)
        k = partial_rope(qkv[:, :, 1], pos)
        o, lse = dilated_group(q, k, qkv[:, :, 2], window, dil)
        outs.append(o)
        lses.append(lse)
    wts = jax.nn.softmax(jnp.stack(lses, axis=0), axis=0)
    o = jnp.einsum('gbth,gbthd->bthd', wts, jnp.stack(outs, axis=0).astype(jnp.float32))
    o = o.astype(h.dtype).reshape(B, T, MIXER_WIDTH)
    rest = h @ w_in[:, len(DIL_GROUPS) * 3 * MIXER_WIDTH:]
    q_mem, gate = jnp.split(rest, [MEM_WIDTH], axis=-1)
    return o, q_mem, gate


def setup_inputs(seed: int = 0) -> dict:
    key = jax.random.key(seed)
    ks = jax.random.split(key, 16)
    f32 = jnp.float32
    nrm = lambda k, shape, scale: jax.random.normal(k, shape, f32) * scale
    return {
        "x": nrm(ks[0], (BATCH, SEQ, D_MODEL), 1.0),
        "mem": nrm(ks[1], (BATCH, N_MEM, D_MODEL), 1.0),
        "mem_norm_w": 1.0 + nrm(ks[2], (D_MODEL,), 0.02),
        "norm_w": 1.0 + nrm(ks[3], (DEPTH, D_MODEL), 0.02),
        "w_memkv": nrm(ks[4], (DEPTH, D_MODEL, 2 * MEM_WIDTH), D_MODEL ** -0.5),
        "w_out": nrm(ks[5], (DEPTH, MIX_WIDTH, D_MODEL), MIX_WIDTH ** -0.5),
        "w_in_a": nrm(ks[6], (N_A, D_MODEL, GLA_IN), D_MODEL ** -0.5),
        "w_gate_up": nrm(ks[7], (N_A, GLA_GATE_RANK, GLA_HK), GLA_GATE_RANK ** -0.5),
        "b_gate": nrm(ks[8], (N_A, GLA_HK), 0.1),
        "gla_norm_w": 1.0 + nrm(ks[9], (N_A, GLA_DV), 0.02),
        "w_in_b": nrm(ks[10], (N_B, D_MODEL, DIL_IN), D_MODEL ** -0.5),
        "final_norm_w": 1.0 + nrm(ks[11], (D_MODEL,), 0.02),
    }


def reference(x, mem, mem_norm_w, norm_w, w_memkv, w_out, w_in_a, w_gate_up, b_gate,
              gla_norm_w, w_in_b, final_norm_w):
    mem_n = rmsnorm(mem, mem_norm_w)
    for i in range(DEPTH):
        h = rmsnorm(x, norm_w[i])
        j = i // N_MIXERS
        if i % N_MIXERS == 0:
            mix, q_mem, gate = gla_mixer(h, w_in_a[j], w_gate_up[j], b_gate[j], gla_norm_w[j])
        else:
            mix, q_mem, gate = dilated_mixer(h, w_in_b[j])
        mem_o = memory_attention(q_mem, mem_n, w_memkv[i])
        branch = jnp.concatenate([mix, mem_o], axis=-1) * jax.nn.silu(gate)
        x = x + branch @ w_out[i]
    return rmsnorm(x, final_norm_w)
```

```python
import functools

import jax
import jax.numpy as jnp
from jax import lax
from jax.experimental import pallas as pl
from jax.experimental.pallas import tpu as pltpu

F32 = jnp.float32
BF16 = jnp.bfloat16

LANE = 128
NORM_EPS = 1e-6
MEM_HEADS = 4
MEM_HEAD_DIM = 64
MEM_WIDTH = MEM_HEADS * MEM_HEAD_DIM
GLA_HEADS = 4
GLA_DK = 96
GLA_DV = 192
GLA_DK_PAD = 128
GLA_DV_PAD = 256
GLA_RANK = 16
GLA_TAU = 16.0
GLA_CHUNK = 64
DIL_GROUPS = ((128, 1), (512, 4), (2048, 16))
DIL_HEADS = 6
DIL_HEAD_DIM = 128
DIL_BLOCK = 128
ROPE_THETA = 500000.0
ROPE_DIM = 32
VMEM_LIMIT = 56 * 1024 * 1024

_NT = (((1,), (1,)), ((), ()))
_TN = (((0,), (0,)), ((), ()))


def _dot(a, b):
    return jnp.dot(a, b, preferred_element_type=F32)


def _dot_nt(a, b):
    return lax.dot_general(a, b, _NT, preferred_element_type=F32)


def _dot_tn(a, b):
    return lax.dot_general(a, b, _TN, preferred_element_type=F32)


def _rmsnorm(x, w):
    ms = jnp.mean(x * x, axis=-1, keepdims=True)
    return x * lax.rsqrt(ms + NORM_EPS) * w


def _silu(g):
    return g / (1.0 + jnp.exp(-g))


def _params(n_parallel):
    return pltpu.CompilerParams(
        dimension_semantics=("parallel",) * n_parallel,
        vmem_limit_bytes=VMEM_LIMIT)


def _memkv_kernel(mem_ref, nw_ref, w_ref, kv_ref, *, depth):
    h = _rmsnorm(mem_ref[0], nw_ref[...]).astype(BF16)
    kv = _dot(h, w_ref[...])
    n_mem = kv.shape[0]
    lane = lax.broadcasted_iota(jnp.int32, (n_mem, MEM_WIDTH), 1)
    for l in range(depth):
        base = l * 2 * MEM_WIDTH
        k = kv[:, base:base + MEM_WIDTH] * (MEM_HEAD_DIM ** -0.5)
        v = kv[:, base + MEM_WIDTH:base + 2 * MEM_WIDTH]
        for hh in range(MEM_HEADS):
            msk = (lane >= hh * MEM_HEAD_DIM) & (lane < (hh + 1) * MEM_HEAD_DIM)
            rows = slice(hh * n_mem, (hh + 1) * n_mem)
            kv_ref[l, 0, 0, rows, :] = jnp.where(msk, k, 0.0).astype(BF16)
            kv_ref[l, 1, 0, rows, :] = jnp.where(msk, v, 0.0).astype(BF16)


def _memkv(mem, mem_norm_w, w_memkv):
    B, n_mem, D = mem.shape
    depth = w_memkv.shape[0]
    w = jnp.transpose(w_memkv, (1, 0, 2)).reshape(D, depth * 2 * MEM_WIDTH).astype(BF16)
    return pl.pallas_call(
        functools.partial(_memkv_kernel, depth=depth),
        grid=(B,),
        in_specs=[
            pl.BlockSpec((1, n_mem, D), lambda b: (b, 0, 0)),
            pl.BlockSpec((1, D), lambda b: (0, 0)),
            pl.BlockSpec((D, depth * 2 * MEM_WIDTH), lambda b: (0, 0)),
        ],
        out_specs=pl.BlockSpec((depth, 2, 1, MEM_HEADS * n_mem, MEM_WIDTH),
                               lambda b: (0, 0, b, 0, 0)),
        out_shape=jax.ShapeDtypeStruct((depth, 2, B, MEM_HEADS * n_mem, MEM_WIDTH), BF16),
        compiler_params=_params(1),
        name="memkv",
    )(mem, mem_norm_w.reshape(1, D), w)


def _inproj_a_kernel(x_ref, nw_ref, w_ref, *out_refs):
    h = _rmsnorm(x_ref[...], nw_ref[...]).astype(BF16)
    c0 = 0
    for o_ref in out_refs:
        cw = o_ref.shape[-1]
        o_ref[...] = _dot(h, w_ref[:, c0:c0 + cw]).astype(BF16)
        c0 += cw


def _inproj_a(x2, norm_w, w, widths, tm):
    M, D = x2.shape
    N = w.shape[1]
    return pl.pallas_call(
        _inproj_a_kernel,
        grid=(M // tm,),
        in_specs=[
            pl.BlockSpec((tm, D), lambda i: (i, 0)),
            pl.BlockSpec((1, D), lambda i: (0, 0)),
            pl.BlockSpec((D, N), lambda i: (0, 0)),
        ],
        out_specs=[pl.BlockSpec((tm, cw), lambda i: (i, 0)) for cw in widths],
        out_shape=[jax.ShapeDtypeStruct((M, cw), BF16) for cw in widths],
        compiler_params=_params(1),
        name="inproj_gla",
    )(x2, norm_w.reshape(1, D), w)


def _inproj_b_kernel(x_ref, nw_ref, w_ref, cos_ref, sin_up_ref, sin_dn_ref,
                     qkv_ref, qm_ref, gate_ref):
    h = _rmsnorm(x_ref[...], nw_ref[...]).astype(BF16)
    hw = DIL_HEADS * DIL_HEAD_DIM
    n_qkv = qkv_ref.shape[0] // DIL_HEADS
    for ci in range(n_qkv):
        acc = _dot(h, w_ref[:, ci * hw:(ci + 1) * hw])
        for hh in range(DIL_HEADS):
            slab = acc[:, hh * DIL_HEAD_DIM:(hh + 1) * DIL_HEAD_DIM]
            if ci % 3 < 2:
                slab = (slab * cos_ref[...]
                        + pltpu.roll(slab, ROPE_DIM // 2, 1) * sin_up_ref[...]
                        + pltpu.roll(slab, LANE - ROPE_DIM // 2, 1) * sin_dn_ref[...])
            qkv_ref[ci * DIL_HEADS + hh] = slab.astype(BF16)
    c0 = n_qkv * hw
    qm_ref[...] = _dot(h, w_ref[:, c0:c0 + MEM_WIDTH]).astype(BF16)
    c0 += MEM_WIDTH
    gate_ref[...] = _dot(h, w_ref[:, c0:]).astype(BF16)


def _rope_tables(T):
    half = ROPE_DIM // 2
    inv = ROPE_THETA ** (-jnp.arange(half, dtype=F32) / half)
    ang = jnp.arange(T).astype(F32)[:, None] * inv[None, :]
    cos, sin = jnp.cos(ang), jnp.sin(ang)
    ones = jnp.ones((T, LANE - ROPE_DIM), F32)
    zeros = jnp.zeros((T, LANE - ROPE_DIM), F32)
    zh = jnp.zeros((T, half), F32)
    cos_t = jnp.concatenate([cos, cos, ones], axis=-1)
    sin_up = jnp.concatenate([zh, sin, zeros], axis=-1)
    sin_dn = jnp.concatenate([-sin, zh, zeros], axis=-1)
    return cos_t, sin_up, sin_dn


def _inproj_b(x2, norm_w, w, T, tm):
    M, D = x2.shape
    N = w.shape[1]
    n_heads = len(DIL_GROUPS) * 3 * DIL_HEADS
    gate_w = N - n_heads * DIL_HEAD_DIM - MEM_WIDTH
    cos_t, sin_up, sin_dn = _rope_tables(T)
    tpb = T // tm
    tab_spec = pl.BlockSpec((tm, LANE), lambda i: (i % tpb, 0))
    return pl.pallas_call(
        _inproj_b_kernel,
        grid=(M // tm,),
        in_specs=[
            pl.BlockSpec((tm, D), lambda i: (i, 0)),
            pl.BlockSpec((1, D), lambda i: (0, 0)),
            pl.BlockSpec((D, N), lambda i: (0, 0), pipeline_mode=pl.Buffered(1)),
            tab_spec, tab_spec, tab_spec,
        ],
        out_specs=[
            pl.BlockSpec((n_heads, tm, DIL_HEAD_DIM), lambda i: (0, i, 0)),
            pl.BlockSpec((tm, MEM_WIDTH), lambda i: (i, 0)),
            pl.BlockSpec((tm, gate_w), lambda i: (i, 0)),
        ],
        out_shape=[
            jax.ShapeDtypeStruct((n_heads, M, DIL_HEAD_DIM), BF16),
            jax.ShapeDtypeStruct((M, MEM_WIDTH), BF16),
            jax.ShapeDtypeStruct((M, gate_w), BF16),
        ],
        compiler_params=_params(1),
        name="inproj_dil",
    )(x2, norm_w.reshape(1, D), w, cos_t, sin_up, sin_dn)


def _gla_kernel(q_ref, k_ref, v_ref, gl_ref, gate_ref, wgu_ref, bg_ref, nw_ref, o_ref,
                la_scr, st_scr):
    T = q_ref.shape[1]
    C = GLA_CHUNK
    z = _dot(gl_ref[0], wgu_ref[...]) + bg_ref[...]
    log_sig = jnp.minimum(z, 0.0) - jnp.log1p(jnp.exp(-jnp.abs(z)))
    la_scr[...] = log_sig * (1.0 / GLA_TAU)
    st_scr[...] = jnp.zeros_like(st_scr)
    row = lax.broadcasted_iota(jnp.int32, (C, C), 0)
    col = lax.broadcasted_iota(jnp.int32, (C, C), 1)
    causal = row >= col
    tri = causal.astype(F32)

    def chunk(n, carry):
        rows = pl.ds(pl.multiple_of(n * C, C), C)
        b = jnp.dot(tri, la_scr[rows, :], precision=lax.Precision.HIGHEST,
                    preferred_element_type=F32)
        b_last = b[C - 1:C, :]
        qc = q_ref[0, rows, :].astype(F32) * (GLA_DK ** -0.5)
        kc = k_ref[0, rows, :].astype(F32)
        vc = v_ref[0, rows, :]
        q_in = (qc * jnp.exp(b)).astype(BF16)
        k_in = (kc * jnp.exp(-b)).astype(BF16)
        k_out = (kc * jnp.exp(b_last - b)).astype(BF16)
        a = jnp.where(causal, _dot_nt(q_in, k_in), 0.0)
        st = st_scr[...]
        o = _dot(a.astype(BF16), vc) + _dot_nt(q_in, st.astype(BF16))
        st_scr[...] = st * jnp.exp(b_last) + _dot_tn(vc, k_out)
        ms = jnp.sum(o * o, axis=-1, keepdims=True) * (1.0 / GLA_DV)
        y = o * lax.rsqrt(ms + NORM_EPS) * nw_ref[...]
        g = gate_ref[0, rows, :].astype(F32)
        o_ref[0, rows, :] = (y * _silu(g)).astype(BF16)
        return carry

    lax.fori_loop(0, T // C, chunk, 0)


def _gla(q, k, v, glow, gate, wgu, bg, nw, B, T):
    kw, vw = GLA_DK_PAD, GLA_DV_PAD
    q3 = q.reshape(B, T, GLA_HEADS * kw)
    k3 = k.reshape(B, T, GLA_HEADS * kw)
    v3 = v.reshape(B, T, GLA_HEADS * vw)
    g3 = glow.reshape(B, T, LANE)
    gate3 = gate.reshape(B, T, gate.shape[-1])
    return pl.pallas_call(
        _gla_kernel,
        grid=(B, GLA_HEADS),
        in_specs=[
            pl.BlockSpec((1, T, kw), lambda b, h: (b, 0, h)),
            pl.BlockSpec((1, T, kw), lambda b, h: (b, 0, h)),
            pl.BlockSpec((1, T, vw), lambda b, h: (b, 0, h)),
            pl.BlockSpec((1, T, LANE), lambda b, h: (b, 0, 0)),
            pl.BlockSpec((1, T, vw), lambda b, h: (b, 0, h)),
            pl.BlockSpec((LANE, kw), lambda b, h: (0, h)),
            pl.BlockSpec((1, kw), lambda b, h: (0, h)),
            pl.BlockSpec((1, vw), lambda b, h: (0, 0)),
        ],
        out_specs=pl.BlockSpec((1, T, vw), lambda b, h: (b, 0, h)),
        out_shape=jax.ShapeDtypeStruct((B, T, GLA_HEADS * vw), BF16),
        scratch_shapes=[pltpu.VMEM((T, kw), F32), pltpu.VMEM((vw, kw), F32)],
        compiler_params=_params(2),
        name="gla",
    )(q3, k3, v3, g3, gate3, wgu, bg, nw)


def _dil_kernel(q0, k0, v0, q1, k1, v1, q2, k2, v2, gate_ref, o_ref, o_scr, l_scr):
    T = o_ref.shape[1]
    nb_rows = DIL_BLOCK
    scale = DIL_HEAD_DIM ** -0.5
    qi = lax.broadcasted_iota(jnp.int32, (nb_rows, 2 * nb_rows), 0)
    ki = lax.broadcasted_iota(jnp.int32, (nb_rows, 2 * nb_rows), 1)
    band = (ki >= qi) & (ki <= qi + nb_rows)
    causal = (lax.broadcasted_iota(jnp.int32, (nb_rows, nb_rows), 1)
              <= lax.broadcasted_iota(jnp.int32, (nb_rows, nb_rows), 0))

    def attend(q, kw, vw, mask):
        s = jnp.where(mask, _dot_nt(q, kw) * scale, -jnp.inf)
        m = jnp.max(s, axis=-1, keepdims=True)
        p = jnp.exp(s - m)
        den = jnp.sum(p, axis=-1, keepdims=True)
        o = _dot(p.astype(BF16), vw) * (1.0 / den)
        lse = m + jnp.log(den)
        return o, jnp.broadcast_to(lse, o.shape)

    groups = ((q0, k0, v0), (q1, k1, v1), (q2, k2, v2))
    for g, (qr, kr, vr) in enumerate(groups):
        r = DIL_GROUPS[g][1]
        nb = T // r // nb_rows

        def put(row0, p, o, lse, g=g, r=r):
            if r == 1:
                dst = pl.ds(row0, nb_rows)
            else:
                dst = pl.ds(row0 * r + p, nb_rows, stride=r)
            o_scr[g, dst, :] = o
            l_scr[g, dst, :] = lse

        for p in range(r):
            cols = slice(p * DIL_HEAD_DIM, (p + 1) * DIL_HEAD_DIM)
            first = slice(0, nb_rows)
            o, lse = attend(qr[0, 0, first, cols], kr[0, 0, first, cols],
                            vr[0, 0, first, cols], causal)
            put(0, p, o, lse)

            def block(j, carry, qr=qr, kr=kr, vr=vr, cols=cols, p=p, put=put):
                row0 = pl.multiple_of(j * nb_rows, nb_rows)
                win = pl.ds(pl.multiple_of(row0 - nb_rows, nb_rows), 2 * nb_rows)
                o, lse = attend(qr[0, 0, pl.ds(row0, nb_rows), cols],
                                kr[0, 0, win, cols], vr[0, 0, win, cols], band)
                put(row0, p, o, lse)
                return carry

            if nb > 1:
                lax.fori_loop(1, nb, block, 0)

    rc = 256

    def combine(c, carry):
        rows = pl.ds(pl.multiple_of(c * rc, rc), rc)
        l0, l1, l2 = l_scr[0, rows, :], l_scr[1, rows, :], l_scr[2, rows, :]
        mx = jnp.maximum(jnp.maximum(l0, l1), l2)
        e0, e1, e2 = jnp.exp(l0 - mx), jnp.exp(l1 - mx), jnp.exp(l2 - mx)
        num = e0 * o_scr[0, rows, :] + e1 * o_scr[1, rows, :] + e2 * o_scr[2, rows, :]
        o = num / (e0 + e1 + e2)
        gt = gate_ref[0, rows, :].astype(F32)
        o_ref[0, rows, :] = (o * _silu(gt)).astype(BF16)
        return carry

    lax.fori_loop(0, T // rc, combine, 0)


def _dilated(qkv, gate, B, T):
    n_heads, M, hd = qkv.shape
    in_specs, args = [], []
    for g, (_, r) in enumerate(DIL_GROUPS):
        view = qkv.reshape(n_heads, B, T // r, r * hd)
        for s in range(3):
            base = (g * 3 + s) * DIL_HEADS
            in_specs.append(pl.BlockSpec((1, 1, T // r, r * hd),
                                         lambda b, h, base=base: (base + h, b, 0, 0)))
            args.append(view)
    in_specs.append(pl.BlockSpec((1, T, hd), lambda b, h: (b, 0, h)))
    args.append(gate.reshape(B, T, gate.shape[-1]))
    n_groups = len(DIL_GROUPS)
    return pl.pallas_call(
        _dil_kernel,
        grid=(B, DIL_HEADS),
        in_specs=in_specs,
        out_specs=pl.BlockSpec((1, T, hd), lambda b, h: (b, 0, h)),
        out_shape=jax.ShapeDtypeStruct((B, T, DIL_HEADS * hd), BF16),
        scratch_shapes=[pltpu.VMEM((n_groups, T, hd), F32), pltpu.VMEM((n_groups, T, hd), F32)],
        compiler_params=_params(2),
        name="dilated",
    )(*args)


def _out_kernel(mix_ref, qm_ref, gm_ref, kst_ref, vst_ref, wmix_ref, wmem_ref, x_ref, fnw_ref,
                o_ref, *, final):
    n_mem = kst_ref.shape[0] // MEM_HEADS
    s = _dot_nt(qm_ref[...], kst_ref[...])
    mo = None
    for hh in range(MEM_HEADS):
        seg = s[:, hh * n_mem:(hh + 1) * n_mem]
        m = jnp.max(seg, axis=-1, keepdims=True)
        e = jnp.exp(seg - m)
        p = (e / jnp.sum(e, axis=-1, keepdims=True)).astype(BF16)
        part = _dot(p, vst_ref[hh * n_mem:(hh + 1) * n_mem, :])
        mo = part if mo is None else mo + part
    bm = (mo * _silu(gm_ref[...].astype(F32))).astype(BF16)
    y = _dot(mix_ref[...], wmix_ref[...]) + _dot(bm, wmem_ref[...]) + x_ref[...]
    if final:
        y = _rmsnorm(y, fnw_ref[...])
    o_ref[...] = y


def _out_proj(mix2, qm2, gate2, kv, layer, w_mix, w_mem, x2, fnw, T, tm, final):
    M, D = x2.shape
    wm = mix2.shape[1]
    gate_blk = (gate2.shape[1] - MEM_WIDTH) // MEM_WIDTH
    tpb = T // tm
    rows_kv = kv.shape[3]
    kv_spec = lambda which: pl.BlockSpec(
        (None, None, None, rows_kv, MEM_WIDTH), lambda i: (layer, which, i // tpb, 0, 0))
    return pl.pallas_call(
        functools.partial(_out_kernel, final=final),
        grid=(M // tm,),
        in_specs=[
            pl.BlockSpec((tm, wm), lambda i: (i, 0)),
            pl.BlockSpec((tm, MEM_WIDTH), lambda i: (i, 0)),
            pl.BlockSpec((tm, MEM_WIDTH), lambda i: (i, gate_blk)),
            kv_spec(0), kv_spec(1),
            pl.BlockSpec((wm, D), lambda i: (0, 0)),
            pl.BlockSpec((MEM_WIDTH, D), lambda i: (0, 0)),
            pl.BlockSpec((tm, D), lambda i: (i, 0)),
            pl.BlockSpec((1, D), lambda i: (0, 0)),
        ],
        out_specs=pl.BlockSpec((tm, D), lambda i: (i, 0)),
        out_shape=jax.ShapeDtypeStruct((M, D), F32),
        compiler_params=_params(1),
        name="memattn_outproj",
    )(mix2, qm2, gate2, kv, kv, w_mix, w_mem, x2, fnw.reshape(1, D))


def _pad_heads(w, heads, width, padded, axis=-1):
    axis = axis % w.ndim
    shape = w.shape[:axis] + (heads, width) + w.shape[axis + 1:]
    pads = [(0, 0)] * (w.ndim + 1)
    pads[axis + 1] = (0, padded - width)
    out = jnp.pad(w.reshape(shape), pads)
    return out.reshape(w.shape[:axis] + (heads * padded,) + w.shape[axis + 1:])


def _gla_weights(w_in, w_gate_up, b_gate, gla_norm_w, w_out):
    hk = GLA_HEADS * GLA_DK
    mixw = GLA_HEADS * GLA_DV
    c = [0, hk, 2 * hk, 2 * hk + mixw, 2 * hk + mixw + GLA_RANK, 2 * hk + mixw + GLA_RANK + MEM_WIDTH]
    q, k, v = w_in[:, c[0]:c[1]], w_in[:, c[1]:c[2]], w_in[:, c[2]:c[3]]
    gl, qm, gate = w_in[:, c[3]:c[4]], w_in[:, c[4]:c[5]], w_in[:, c[5]:]
    w = jnp.concatenate([
        _pad_heads(q, GLA_HEADS, GLA_DK, GLA_DK_PAD),
        _pad_heads(k, GLA_HEADS, GLA_DK, GLA_DK_PAD),
        _pad_heads(v, GLA_HEADS, GLA_DV, GLA_DV_PAD),
        jnp.pad(gl, ((0, 0), (0, LANE - GLA_RANK))),
        qm,
        _pad_heads(gate[:, :mixw], GLA_HEADS, GLA_DV, GLA_DV_PAD),
        gate[:, mixw:],
    ], axis=1).astype(BF16)
    widths = (GLA_HEADS * GLA_DK_PAD, GLA_HEADS * GLA_DK_PAD, GLA_HEADS * GLA_DV_PAD, LANE,
              MEM_WIDTH, GLA_HEADS * GLA_DV_PAD + MEM_WIDTH)
    wgu = jnp.pad(_pad_heads(w_gate_up, GLA_HEADS, GLA_DK, GLA_DK_PAD),
                  ((0, LANE - GLA_RANK), (0, 0))).astype(BF16)
    bg = _pad_heads(b_gate.reshape(1, hk), GLA_HEADS, GLA_DK, GLA_DK_PAD)
    nw = jnp.pad(gla_norm_w.reshape(1, GLA_DV), ((0, 0), (0, GLA_DV_PAD - GLA_DV)))
    w_mix = _pad_heads(w_out[:mixw], GLA_HEADS, GLA_DV, GLA_DV_PAD, axis=0).astype(BF16)
    w_mem = w_out[mixw:].astype(BF16)
    return w, widths, wgu, bg, nw, w_mix, w_mem


def kernel(x, mem, mem_norm_w, norm_w, w_memkv, w_out, w_in_a, w_gate_up, b_gate, gla_norm_w,
           w_in_b, final_norm_w):
    B, T, D = x.shape
    depth = norm_w.shape[0]
    M = B * T
    tm = 512
    kv = _memkv(mem, mem_norm_w, w_memkv)
    x2 = x.reshape(M, D)
    for i in range(depth):
        j = i // 2
        final = i == depth - 1
        if i % 2 == 0:
            w, widths, wgu, bg, nw, w_mix, w_mem = _gla_weights(
                w_in_a[j], w_gate_up[j], b_gate[j], gla_norm_w[j], w_out[i])
            q, k, v, glow, qm, gate = _inproj_a(x2, norm_w[i], w, widths, tm)
            mix = _gla(q, k, v, glow, gate, wgu, bg, nw, B, T)
        else:
            mixw = DIL_HEADS * DIL_HEAD_DIM
            w_mix = w_out[i][:mixw].astype(BF16)
            w_mem = w_out[i][mixw:].astype(BF16)
            qkv, qm, gate = _inproj_b(x2, norm_w[i], w_in_b[j].astype(BF16), T, tm)
            mix = _dilated(qkv, gate, B, T)
        x2 = _out_proj(mix.reshape(M, -1), qm, gate, kv, i, w_mix, w_mem, x2, final_norm_w,
                       T, tm, final)
    return x2.reshape(B, T, D)
```

```python
import functools

import jax
import jax.numpy as jnp
from jax import lax
from jax.experimental import pallas as pl
from jax.experimental.pallas import tpu as pltpu

F32 = jnp.float32
BF16 = jnp.bfloat16

LANE = 128
NORM_EPS = 1e-6
MEM_HEADS = 4
MEM_HEAD_DIM = 64
MEM_WIDTH = MEM_HEADS * MEM_HEAD_DIM
GLA_HEADS = 4
GLA_DK = 96
GLA_DV = 192
GLA_DK_PAD = 128
GLA_DV_PAD = 256
GLA_RANK = 16
GLA_TAU = 16.0
GLA_CHUNK = 64
GLA_UNROLL = 4
DIL_GROUPS = ((128, 1), (512, 4), (2048, 16))
DIL_HEADS = 6
DIL_HEAD_DIM = 128
DIL_BLOCK = 128
DIL_BATCH = 4
ROPE_THETA = 500000.0
ROPE_DIM = 32
VMEM_LIMIT = 56 * 1024 * 1024

_NT = (((1,), (1,)), ((), ()))
_TN = (((0,), (0,)), ((), ()))


def _dot(a, b):
    return jnp.dot(a, b, preferred_element_type=F32)


def _dot_nt(a, b):
    return lax.dot_general(a, b, _NT, preferred_element_type=F32)


def _dot_tn(a, b):
    return lax.dot_general(a, b, _TN, preferred_element_type=F32)


def _rmsnorm(x, w):
    ms = jnp.mean(x * x, axis=-1, keepdims=True)
    return x * lax.rsqrt(ms + NORM_EPS) * w


def _silu(g):
    return g / (1.0 + jnp.exp(-g))


def _params(n_parallel):
    return pltpu.CompilerParams(
        dimension_semantics=("parallel",) * n_parallel,
        vmem_limit_bytes=VMEM_LIMIT)


def _memkv_kernel(mem_ref, nw_ref, w_ref, kv_ref, *, depth):
    h = _rmsnorm(mem_ref[0], nw_ref[...]).astype(BF16)
    kv = _dot(h, w_ref[...])
    n_mem = kv.shape[0]
    lane = lax.broadcasted_iota(jnp.int32, (n_mem, MEM_WIDTH), 1)
    for l in range(depth):
        base = l * 2 * MEM_WIDTH
        k = kv[:, base:base + MEM_WIDTH] * (MEM_HEAD_DIM ** -0.5)
        v = kv[:, base + MEM_WIDTH:base + 2 * MEM_WIDTH]
        for hh in range(MEM_HEADS):
            msk = (lane >= hh * MEM_HEAD_DIM) & (lane < (hh + 1) * MEM_HEAD_DIM)
            rows = slice(hh * n_mem, (hh + 1) * n_mem)
            kv_ref[l, 0, 0, rows, :] = jnp.where(msk, k, 0.0).astype(BF16)
            kv_ref[l, 1, 0, rows, :] = jnp.where(msk, v, 0.0).astype(BF16)


def _memkv(mem, mem_norm_w, w_memkv):
    B, n_mem, D = mem.shape
    depth = w_memkv.shape[0]
    w = jnp.transpose(w_memkv, (1, 0, 2)).reshape(D, depth * 2 * MEM_WIDTH).astype(BF16)
    return pl.pallas_call(
        functools.partial(_memkv_kernel, depth=depth),
        grid=(B,),
        in_specs=[
            pl.BlockSpec((1, n_mem, D), lambda b: (b, 0, 0)),
            pl.BlockSpec((1, D), lambda b: (0, 0)),
            pl.BlockSpec((D, depth * 2 * MEM_WIDTH), lambda b: (0, 0)),
        ],
        out_specs=pl.BlockSpec((depth, 2, 1, MEM_HEADS * n_mem, MEM_WIDTH),
                               lambda b: (0, 0, b, 0, 0)),
        out_shape=jax.ShapeDtypeStruct((depth, 2, B, MEM_HEADS * n_mem, MEM_WIDTH), BF16),
        compiler_params=_params(1),
        name="memkv",
    )(mem, mem_norm_w.reshape(1, D), w)


def _inproj_a_kernel(x_ref, nw_ref, w_ref, *out_refs):
    h = _rmsnorm(x_ref[...], nw_ref[...]).astype(BF16)
    c0 = 0
    for o_ref in out_refs:
        cw = o_ref.shape[-1]
        o_ref[...] = _dot(h, w_ref[:, c0:c0 + cw]).astype(BF16)
        c0 += cw


def _inproj_a(x2, norm_w, w, widths, tm):
    M, D = x2.shape
    N = w.shape[1]
    return pl.pallas_call(
        _inproj_a_kernel,
        grid=(M // tm,),
        in_specs=[
            pl.BlockSpec((tm, D), lambda i: (i, 0)),
            pl.BlockSpec((1, D), lambda i: (0, 0)),
            pl.BlockSpec((D, N), lambda i: (0, 0)),
        ],
        out_specs=[pl.BlockSpec((tm, cw), lambda i: (i, 0)) for cw in widths],
        out_shape=[jax.ShapeDtypeStruct((M, cw), BF16) for cw in widths],
        compiler_params=_params(1),
        name="inproj_gla",
    )(x2, norm_w.reshape(1, D), w)


def _inproj_b_kernel(x_ref, nw_ref, w_ref, cos_ref, sin_up_ref, sin_dn_ref,
                     qkv0_ref, qkv1_ref, qkv2_ref, qm_ref, gate_ref, acc_scr):
    h = _rmsnorm(x_ref[...], nw_ref[...]).astype(BF16)
    tm = x_ref.shape[0]
    hd = DIL_HEAD_DIM
    hw = DIL_HEADS * hd
    group_refs = (qkv0_ref, qkv1_ref, qkv2_ref)
    for g, (_, r) in enumerate(DIL_GROUPS):
        out_ref = group_refs[g]
        for s in range(3):
            ci = g * 3 + s
            acc = _dot(h, w_ref[:, ci * hw:(ci + 1) * hw])
            for hh in range(DIL_HEADS):
                slab = acc[:, hh * hd:(hh + 1) * hd]
                if s < 2:
                    slab = (slab * cos_ref[...]
                            + pltpu.roll(slab, ROPE_DIM // 2, 1) * sin_up_ref[...]
                            + pltpu.roll(slab, LANE - ROPE_DIM // 2, 1) * sin_dn_ref[...])
                if r == 1:
                    out_ref[s * DIL_HEADS + hh] = slab.astype(BF16)
                else:
                    acc_scr[hh] = slab
            if r > 1:
                for hh in range(DIL_HEADS):
                    for p in range(r):
                        rows = acc_scr[hh, pl.ds(p, tm // r, stride=r), :]
                        out_ref[s * DIL_HEADS + hh, :, p * hd:(p + 1) * hd] = rows.astype(BF16)
    c0 = len(DIL_GROUPS) * 3 * hw
    qm_ref[...] = _dot(h, w_ref[:, c0:c0 + MEM_WIDTH]).astype(BF16)
    c0 += MEM_WIDTH
    gate_ref[...] = _dot(h, w_ref[:, c0:]).astype(BF16)


def _rope_tables(T):
    half = ROPE_DIM // 2
    inv = ROPE_THETA ** (-jnp.arange(half, dtype=F32) / half)
    ang = jnp.arange(T).astype(F32)[:, None] * inv[None, :]
    cos, sin = jnp.cos(ang), jnp.sin(ang)
    ones = jnp.ones((T, LANE - ROPE_DIM), F32)
    zeros = jnp.zeros((T, LANE - ROPE_DIM), F32)
    zh = jnp.zeros((T, half), F32)
    cos_t = jnp.concatenate([cos, cos, ones], axis=-1)
    sin_up = jnp.concatenate([zh, sin, zeros], axis=-1)
    sin_dn = jnp.concatenate([-sin, zh, zeros], axis=-1)
    return cos_t, sin_up, sin_dn


def _inproj_b(x2, norm_w, w, T, tm):
    M, D = x2.shape
    N = w.shape[1]
    hd = DIL_HEAD_DIM
    heads_per_group = 3 * DIL_HEADS
    gate_w = N - len(DIL_GROUPS) * heads_per_group * hd - MEM_WIDTH
    cos_t, sin_up, sin_dn = _rope_tables(T)
    tpb = T // tm
    tab_spec = pl.BlockSpec((tm, LANE), lambda i: (i % tpb, 0))
    qkv_specs = [pl.BlockSpec((heads_per_group, tm // r, r * hd), lambda i: (0, i, 0))
                 for _, r in DIL_GROUPS]
    qkv_shapes = [jax.ShapeDtypeStruct((heads_per_group, M // r, r * hd), BF16)
                  for _, r in DIL_GROUPS]
    return pl.pallas_call(
        _inproj_b_kernel,
        grid=(M // tm,),
        in_specs=[
            pl.BlockSpec((tm, D), lambda i: (i, 0)),
            pl.BlockSpec((1, D), lambda i: (0, 0)),
            pl.BlockSpec((D, N), lambda i: (0, 0), pipeline_mode=pl.Buffered(1)),
            tab_spec, tab_spec, tab_spec,
        ],
        out_specs=qkv_specs + [
            pl.BlockSpec((tm, MEM_WIDTH), lambda i: (i, 0)),
            pl.BlockSpec((tm, gate_w), lambda i: (i, 0)),
        ],
        out_shape=qkv_shapes + [
            jax.ShapeDtypeStruct((M, MEM_WIDTH), BF16),
            jax.ShapeDtypeStruct((M, gate_w), BF16),
        ],
        scratch_shapes=[pltpu.VMEM((DIL_HEADS, tm, hd), F32)],
        compiler_params=_params(1),
        name="inproj_dil",
    )(x2, norm_w.reshape(1, D), w, cos_t, sin_up, sin_dn)


def _gla_kernel(q_ref, k_ref, v_ref, gl_ref, gate_ref, wgu_ref, bg_ref, nw_ref, o_ref):
    T = q_ref.shape[1]
    C = GLA_CHUNK
    U = GLA_UNROLL
    R = U * C
    row = lax.broadcasted_iota(jnp.int32, (R, R), 0)
    col = lax.broadcasted_iota(jnp.int32, (R, R), 1)
    shift = C.bit_length() - 1
    in_chunk_causal = ((row >> shift) == (col >> shift)) & (row >= col)
    tri = jnp.where(in_chunk_causal, 1.0, 0.0).astype(BF16)

    def body(i, st):
        rows = pl.ds(pl.multiple_of(i * R, R), R)
        z = _dot(gl_ref[0, rows, :], wgu_ref[...]) + bg_ref[...]
        la = (jnp.minimum(z, 0.0) - jnp.log1p(jnp.exp(-jnp.abs(z)))) * (1.0 / GLA_TAU)
        hi = la.astype(BF16)
        rem = la - hi.astype(F32)
        mid = rem.astype(BF16)
        lo = (rem - mid.astype(F32)).astype(BF16)
        b = _dot(tri, hi) + _dot(tri, mid) + _dot(tri, lo)
        b_last = [b[c * C + C - 1:(c + 1) * C, :] for c in range(U)]
        bl_rows = jnp.concatenate([jnp.broadcast_to(bl, (C, bl.shape[-1])) for bl in b_last], axis=0)
        qc = q_ref[0, rows, :].astype(F32) * (GLA_DK ** -0.5)
        kc = k_ref[0, rows, :].astype(F32)
        vc = v_ref[0, rows, :]
        q_in = (qc * jnp.exp(b)).astype(BF16)
        k_in = (kc * jnp.exp(-b)).astype(BF16)
        k_out = (kc * jnp.exp(bl_rows - b)).astype(BF16)
        a = jnp.where(in_chunk_causal, _dot_nt(q_in, k_in), 0.0).astype(BF16)
        o_intra = _dot(a, vc)
        o_inter = []
        for c in range(U):
            sl = slice(c * C, (c + 1) * C)
            o_inter.append(_dot_nt(q_in[sl], st.astype(BF16)))
            st = st * jnp.exp(b_last[c]) + _dot_tn(vc[sl], k_out[sl])
        o = o_intra + jnp.concatenate(o_inter, axis=0)
        ms = jnp.sum(o * o, axis=-1, keepdims=True) * (1.0 / GLA_DV)
        y = o * lax.rsqrt(ms + NORM_EPS) * nw_ref[...]
        g = gate_ref[0, rows, :].astype(F32)
        o_ref[0, rows, :] = (y * _silu(g)).astype(BF16)
        return st

    lax.fori_loop(0, T // R, body, jnp.zeros((v_ref.shape[-1], q_ref.shape[-1]), F32), unroll=True)


def _gla(q, k, v, glow, gate, wgu, bg, nw, B, T):
    kw, vw = GLA_DK_PAD, GLA_DV_PAD
    q3 = q.reshape(B, T, GLA_HEADS * kw)
    k3 = k.reshape(B, T, GLA_HEADS * kw)
    v3 = v.reshape(B, T, GLA_HEADS * vw)
    g3 = glow.reshape(B, T, LANE)
    gate3 = gate.reshape(B, T, gate.shape[-1])
    return pl.pallas_call(
        _gla_kernel,
        grid=(B, GLA_HEADS),
        in_specs=[
            pl.BlockSpec((1, T, kw), lambda b, h: (b, 0, h)),
            pl.BlockSpec((1, T, kw), lambda b, h: (b, 0, h)),
            pl.BlockSpec((1, T, vw), lambda b, h: (b, 0, h)),
            pl.BlockSpec((1, T, LANE), lambda b, h: (b, 0, 0)),
            pl.BlockSpec((1, T, vw), lambda b, h: (b, 0, h)),
            pl.BlockSpec((LANE, kw), lambda b, h: (0, h)),
            pl.BlockSpec((1, kw), lambda b, h: (0, h)),
            pl.BlockSpec((1, vw), lambda b, h: (0, 0)),
        ],
        out_specs=pl.BlockSpec((1, T, vw), lambda b, h: (b, 0, h)),
        out_shape=jax.ShapeDtypeStruct((B, T, GLA_HEADS * vw), BF16),
        compiler_params=_params(2),
        name="gla",
    )(q3, k3, v3, g3, gate3, wgu, bg, nw)


def _dil_kernel(q0, k0, v0, q1, k1, v1, q2, k2, v2, gate_ref, o_ref, o_scr, l_scr):
    T = o_ref.shape[1]
    n = DIL_BLOCK
    hd = DIL_HEAD_DIM
    scale = hd ** -0.5
    qi = lax.broadcasted_iota(jnp.int32, (n, 2 * n), 0)
    ki = lax.broadcasted_iota(jnp.int32, (n, 2 * n), 1)
    band = (ki >= qi) & (ki <= qi + n)
    causal = (lax.broadcasted_iota(jnp.int32, (n, n), 1)
              <= lax.broadcasted_iota(jnp.int32, (n, n), 0))
    groups = ((q0, k0, v0), (q1, k1, v1), (q2, k2, v2))

    def run(tasks):
        loaded = []
        for g, p, row0, first in tasks:
            qr, kr, vr = groups[g]
            cols = slice(p * hd, (p + 1) * hd)
            if first:
                win = pl.ds(0, n)
            else:
                win = pl.ds(pl.multiple_of(row0 - n, n), 2 * n)
            loaded.append((qr[0, 0, pl.ds(row0, n), cols], kr[0, 0, win, cols], vr[0, 0, win, cols],
                           causal if first else band))
        s = [jnp.where(mask, _dot_nt(q, kw) * scale, -jnp.inf) for q, kw, _, mask in loaded]
        m = [jnp.max(x, axis=-1, keepdims=True) for x in s]
        e = [jnp.exp(x - mx) for x, mx in zip(s, m)]
        den = [jnp.sum(x, axis=-1, keepdims=True) for x in e]
        o = [_dot(x.astype(BF16), ld[2]) * (1.0 / d) for x, ld, d in zip(e, loaded, den)]
        lse = [mx + jnp.log(d) for mx, d in zip(m, den)]
        for (g, p, row0, _), ov, lv in zip(tasks, o, lse):
            r = DIL_GROUPS[g][1]
            dst = pl.ds(row0, n) if r == 1 else pl.ds(row0 * r + p, n, stride=r)
            o_scr[g, dst, :] = ov
            l_scr[g, dst, :] = jnp.broadcast_to(lv, ov.shape)

    for g, (_, r) in enumerate(DIL_GROUPS):
        nb = T // r // n
        if nb == 1:
            for p0 in range(0, r, DIL_BATCH):
                run([(g, p, 0, True) for p in range(p0, min(p0 + DIL_BATCH, r))])
        elif r == 1:
            run([(g, 0, 0, True)])
            per = 3
            assert (nb - 1) % per == 0

            def blocks(i, carry, g=g, per=per):
                j0 = 1 + i * per
                run([(g, 0, pl.multiple_of((j0 + d) * n, n), False) for d in range(per)])
                return carry

            lax.fori_loop(0, (nb - 1) // per, blocks, 0)
        else:
            assert r <= DIL_BATCH
            run([(g, p, 0, True) for p in range(r)])

            def phases(j, carry, g=g, r=r):
                row0 = pl.multiple_of(j * n, n)
                run([(g, p, row0, False) for p in range(r)])
                return carry

            lax.fori_loop(1, nb, phases, 0)

    rc = 256

    def combine(c, carry):
        rows = pl.ds(pl.multiple_of(c * rc, rc), rc)
        l0, l1, l2 = l_scr[0, rows, :], l_scr[1, rows, :], l_scr[2, rows, :]
        mx = jnp.maximum(jnp.maximum(l0, l1), l2)
        e0, e1, e2 = jnp.exp(l0 - mx), jnp.exp(l1 - mx), jnp.exp(l2 - mx)
        num = e0 * o_scr[0, rows, :] + e1 * o_scr[1, rows, :] + e2 * o_scr[2, rows, :]
        o = num / (e0 + e1 + e2)
        gt = gate_ref[0, rows, :].astype(F32)
        o_ref[0, rows, :] = (o * _silu(gt)).astype(BF16)
        return carry

    lax.fori_loop(0, T // rc, combine, 0)


def _dilated(qkv_groups, gate, B, T):
    hd = DIL_HEAD_DIM
    in_specs, args = [], []
    for (_, r), qkv in zip(DIL_GROUPS, qkv_groups):
        view = qkv.reshape(qkv.shape[0], B, T // r, r * hd)
        for s in range(3):
            in_specs.append(pl.BlockSpec((1, 1, T // r, r * hd),
                                         lambda b, h, s=s: (s * DIL_HEADS + h, b, 0, 0)))
            args.append(view)
    in_specs.append(pl.BlockSpec((1, T, hd), lambda b, h: (b, 0, h)))
    args.append(gate.reshape(B, T, gate.shape[-1]))
    n_groups = len(DIL_GROUPS)
    return pl.pallas_call(
        _dil_kernel,
        grid=(B, DIL_HEADS),
        in_specs=in_specs,
        out_specs=pl.BlockSpec((1, T, hd), lambda b, h: (b, 0, h)),
        out_shape=jax.ShapeDtypeStruct((B, T, DIL_HEADS * hd), BF16),
        scratch_shapes=[pltpu.VMEM((n_groups, T, hd), F32), pltpu.VMEM((n_groups, T, hd), F32)],
        compiler_params=_params(2),
        name="dilated",
    )(*args)


def _out_kernel(mix_ref, qm_ref, gm_ref, kst_ref, vst_ref, wmix_ref, wmem_ref, x_ref, fnw_ref,
                o_ref, *, final):
    n_mem = kst_ref.shape[0] // MEM_HEADS
    s = _dot_nt(qm_ref[...], kst_ref[...])
    mo = None
    for hh in range(MEM_HEADS):
        seg = s[:, hh * n_mem:(hh + 1) * n_mem]
        m = jnp.max(seg, axis=-1, keepdims=True)
        e = jnp.exp(seg - m)
        p = (e / jnp.sum(e, axis=-1, keepdims=True)).astype(BF16)
        part = _dot(p, vst_ref[hh * n_mem:(hh + 1) * n_mem, :])
        mo = part if mo is None else mo + part
    bm = (mo * _silu(gm_ref[...].astype(F32))).astype(BF16)
    y = _dot(mix_ref[...], wmix_ref[...]) + _dot(bm, wmem_ref[...]) + x_ref[...]
    if final:
        y = _rmsnorm(y, fnw_ref[...])
    o_ref[...] = y


def _out_proj(mix2, qm2, gate2, kv, layer, w_mix, w_mem, x2, fnw, T, tm, final):
    M, D = x2.shape
    wm = mix2.shape[1]
    gate_blk = (gate2.shape[1] - MEM_WIDTH) // MEM_WIDTH
    tpb = T // tm
    rows_kv = kv.shape[3]
    kv_spec = lambda which: pl.BlockSpec(
        (None, None, None, rows_kv, MEM_WIDTH), lambda i: (layer, which, i // tpb, 0, 0))
    return pl.pallas_call(
        functools.partial(_out_kernel, final=final),
        grid=(M // tm,),
        in_specs=[
            pl.BlockSpec((tm, wm), lambda i: (i, 0)),
            pl.BlockSpec((tm, MEM_WIDTH), lambda i: (i, 0)),
            pl.BlockSpec((tm, MEM_WIDTH), lambda i: (i, gate_blk)),
            kv_spec(0), kv_spec(1),
            pl.BlockSpec((wm, D), lambda i: (0, 0)),
            pl.BlockSpec((MEM_WIDTH, D), lambda i: (0, 0)),
            pl.BlockSpec((tm, D), lambda i: (i, 0)),
            pl.BlockSpec((1, D), lambda i: (0, 0)),
        ],
        out_specs=pl.BlockSpec((tm, D), lambda i: (i, 0)),
        out_shape=jax.ShapeDtypeStruct((M, D), F32),
        compiler_params=_params(1),
        name="memattn_outproj",
    )(mix2, qm2, gate2, kv, kv, w_mix, w_mem, x2, fnw.reshape(1, D))


def _pad_heads(w, heads, width, padded, axis=-1):
    axis = axis % w.ndim
    shape = w.shape[:axis] + (heads, width) + w.shape[axis + 1:]
    pads = [(0, 0)] * (w.ndim + 1)
    pads[axis + 1] = (0, padded - width)
    out = jnp.pad(w.reshape(shape), pads)
    return out.reshape(w.shape[:axis] + (heads * padded,) + w.shape[axis + 1:])


def _gla_weights(w_in, w_gate_up, b_gate, gla_norm_w, w_out):
    hk = GLA_HEADS * GLA_DK
    mixw = GLA_HEADS * GLA_DV
    c = [0, hk, 2 * hk, 2 * hk + mixw, 2 * hk + mixw + GLA_RANK, 2 * hk + mixw + GLA_RANK + MEM_WIDTH]
    q, k, v = w_in[:, c[0]:c[1]], w_in[:, c[1]:c[2]], w_in[:, c[2]:c[3]]
    gl, qm, gate = w_in[:, c[3]:c[4]], w_in[:, c[4]:c[5]], w_in[:, c[5]:]
    w = jnp.concatenate([
        _pad_heads(q, GLA_HEADS, GLA_DK, GLA_DK_PAD),
        _pad_heads(k, GLA_HEADS, GLA_DK, GLA_DK_PAD),
        _pad_heads(v, GLA_HEADS, GLA_DV, GLA_DV_PAD),
        jnp.pad(gl, ((0, 0), (0, LANE - GLA_RANK))),
        qm,
        _pad_heads(gate[:, :mixw], GLA_HEADS, GLA_DV, GLA_DV_PAD),
        gate[:, mixw:],
    ], axis=1).astype(BF16)
    widths = (GLA_HEADS * GLA_DK_PAD, GLA_HEADS * GLA_DK_PAD, GLA_HEADS * GLA_DV_PAD, LANE,
              MEM_WIDTH, GLA_HEADS * GLA_DV_PAD + MEM_WIDTH)
    wgu = jnp.pad(_pad_heads(w_gate_up, GLA_HEADS, GLA_DK, GLA_DK_PAD),
                  ((0, LANE - GLA_RANK), (0, 0))).astype(BF16)
    bg = _pad_heads(b_gate.reshape(1, hk), GLA_HEADS, GLA_DK, GLA_DK_PAD)
    nw = jnp.pad(gla_norm_w.reshape(1, GLA_DV), ((0, 0), (0, GLA_DV_PAD - GLA_DV)))
    w_mix = _pad_heads(w_out[:mixw], GLA_HEADS, GLA_DV, GLA_DV_PAD, axis=0).astype(BF16)
    w_mem = w_out[mixw:].astype(BF16)
    return w, widths, wgu, bg, nw, w_mix, w_mem


def kernel(x, mem, mem_norm_w, norm_w, w_memkv, w_out, w_in_a, w_gate_up, b_gate, gla_norm_w,
           w_in_b, final_norm_w):
    B, T, D = x.shape
    depth = norm_w.shape[0]
    M = B * T
    tm = 512
    kv = _memkv(mem, mem_norm_w, w_memkv)
    x2 = x.reshape(M, D)
    for i in range(depth):
        j = i // 2
        final = i == depth - 1
        if i % 2 == 0:
            w, widths, wgu, bg, nw, w_mix, w_mem = _gla_weights(
                w_in_a[j], w_gate_up[j], b_gate[j], gla_norm_w[j], w_out[i])
            q, k, v, glow, qm, gate = _inproj_a(x2, norm_w[i], w, widths, tm)
            mix = _gla(q, k, v, glow, gate, wgu, bg, nw, B, T)
        else:
            mixw = DIL_HEADS * DIL_HEAD_DIM
            w_mix = w_out[i][:mixw].astype(BF16)
            w_mem = w_out[i][mixw:].astype(BF16)
            *qkv_groups, qm, gate = _inproj_b(x2, norm_w[i], w_in_b[j].astype(BF16), T, tm)
            mix = _dilated(qkv_groups, gate, B, T)
        x2 = _out_proj(mix.reshape(M, -1), qm, gate, kv, i, w_mix, w_mem, x2, final_norm_w,
                       T, tm, final)
    return x2.reshape(B, T, D)
```

```python
import functools

import jax
import jax.numpy as jnp
from jax import lax
from jax.experimental import pallas as pl
from jax.experimental.pallas import tpu as pltpu

F32 = jnp.float32
BF16 = jnp.bfloat16

LANE = 128
NORM_EPS = 1e-6
LOG2_E = 1.4426950408889634
MEM_HEADS = 4
MEM_HEAD_DIM = 64
MEM_WIDTH = MEM_HEADS * MEM_HEAD_DIM
GLA_HEADS = 4
GLA_DK = 96
GLA_DV = 192
GLA_DK_PAD = 128
GLA_DV_PAD = 256
GLA_RANK = 16
GLA_TAU = 16.0
GLA_CHUNK = 64
GLA_UNROLL = 4
DIL_GROUPS = ((128, 1), (512, 4), (2048, 16))
DIL_HEADS = 6
DIL_HEAD_DIM = 128
DIL_BLOCK = 128
DIL_BATCH = 4
OUT_SPLIT = 512
ROPE_THETA = 500000.0
ROPE_DIM = 32
VMEM_LIMIT = 56 * 1024 * 1024

_NT = (((1,), (1,)), ((), ()))
_TN = (((0,), (0,)), ((), ()))


def _dot(a, b):
    return jnp.dot(a, b, preferred_element_type=F32)


def _dot_nt(a, b):
    return lax.dot_general(a, b, _NT, preferred_element_type=F32)


def _dot_tn(a, b):
    return lax.dot_general(a, b, _TN, preferred_element_type=F32)


def _rmsnorm(x, w):
    ms = jnp.mean(x * x, axis=-1, keepdims=True)
    return x * lax.rsqrt(ms + NORM_EPS) * w


def _silu(g):
    return g * (0.5 + 0.5 * jnp.tanh(0.5 * g))


def _params(n_parallel):
    return pltpu.CompilerParams(
        dimension_semantics=("parallel",) * n_parallel,
        vmem_limit_bytes=VMEM_LIMIT)


def _memkv_kernel(mem_ref, nw_ref, w_ref, kv_ref, *, depth):
    h = _rmsnorm(mem_ref[0], nw_ref[...]).astype(BF16)
    kv = _dot(h, w_ref[...])
    n_mem = kv.shape[0]
    lane = lax.broadcasted_iota(jnp.int32, (n_mem, MEM_WIDTH), 1)
    for l in range(depth):
        base = l * 2 * MEM_WIDTH
        k = kv[:, base:base + MEM_WIDTH] * (MEM_HEAD_DIM ** -0.5)
        v = kv[:, base + MEM_WIDTH:base + 2 * MEM_WIDTH]
        for hh in range(MEM_HEADS):
            msk = (lane >= hh * MEM_HEAD_DIM) & (lane < (hh + 1) * MEM_HEAD_DIM)
            rows = slice(hh * n_mem, (hh + 1) * n_mem)
            kv_ref[l, 0, 0, rows, :] = jnp.where(msk, k, 0.0).astype(BF16)
            kv_ref[l, 1, 0, rows, :] = jnp.where(msk, v, 0.0).astype(BF16)


def _memkv(mem, mem_norm_w, w_memkv):
    B, n_mem, D = mem.shape
    depth = w_memkv.shape[0]
    w = jnp.transpose(w_memkv, (1, 0, 2)).reshape(D, depth * 2 * MEM_WIDTH).astype(BF16)
    return pl.pallas_call(
        functools.partial(_memkv_kernel, depth=depth),
        grid=(B,),
        in_specs=[
            pl.BlockSpec((1, n_mem, D), lambda b: (b, 0, 0)),
            pl.BlockSpec((1, D), lambda b: (0, 0)),
            pl.BlockSpec((D, depth * 2 * MEM_WIDTH), lambda b: (0, 0)),
        ],
        out_specs=pl.BlockSpec((depth, 2, 1, MEM_HEADS * n_mem, MEM_WIDTH),
                               lambda b: (0, 0, b, 0, 0)),
        out_shape=jax.ShapeDtypeStruct((depth, 2, B, MEM_HEADS * n_mem, MEM_WIDTH), BF16),
        compiler_params=_params(1),
        name="memkv",
    )(mem, mem_norm_w.reshape(1, D), w)


def _inproj_a_kernel(x_ref, nw_ref, w_ref, *out_refs):
    h = _rmsnorm(x_ref[...], nw_ref[...]).astype(BF16)
    c0 = 0
    for o_ref in out_refs:
        cw = o_ref.shape[-1]
        o_ref[...] = _dot(h, w_ref[:, c0:c0 + cw]).astype(BF16)
        c0 += cw


def _inproj_a(x2, norm_w, w, widths, tm):
    M, D = x2.shape
    N = w.shape[1]
    return pl.pallas_call(
        _inproj_a_kernel,
        grid=(M // tm,),
        in_specs=[
            pl.BlockSpec((tm, D), lambda i: (i, 0)),
            pl.BlockSpec((1, D), lambda i: (0, 0)),
            pl.BlockSpec((D, N), lambda i: (0, 0)),
        ],
        out_specs=[pl.BlockSpec((tm, cw), lambda i: (i, 0)) for cw in widths],
        out_shape=[jax.ShapeDtypeStruct((M, cw), BF16) for cw in widths],
        compiler_params=_params(1),
        name="inproj_gla",
    )(x2, norm_w.reshape(1, D), w)


def _inproj_b_kernel(x_ref, nw_ref, w_ref, rope_ref, qkv0_ref, qkv1_ref, qkv2_ref, qm_ref, gate_ref,
                     h_scr):
    hf = _rmsnorm(x_ref[...], nw_ref[...])
    h = hf.astype(BF16)
    tm, D = hf.shape
    hd = DIL_HEAD_DIM
    hw = DIL_HEADS * hd
    n_lt = D // LANE
    for lt in range(n_lt):
        h_scr[lt] = hf[:, lt * LANE:(lt + 1) * LANE]
    group_refs = (qkv0_ref, qkv1_ref, qkv2_ref)
    for g, (_, r) in enumerate(DIL_GROUPS):
        out_ref = group_refs[g]
        rows_p = tm // r
        if r == 1:
            hg = h
        else:
            hg = jnp.concatenate(
                [jnp.concatenate([h_scr[lt, pl.ds(p, rows_p, stride=r), :] for lt in range(n_lt)],
                                 axis=1).astype(BF16) for p in range(r)], axis=0)
        cos, sin_up, sin_dn = rope_ref[3 * g], rope_ref[3 * g + 1], rope_ref[3 * g + 2]
        for s in range(3):
            ci = g * 3 + s
            acc = _dot(hg, w_ref[:, ci * hw:(ci + 1) * hw])
            for hh in range(DIL_HEADS):
                slab = acc[:, hh * hd:(hh + 1) * hd]
                if s < 2:
                    slab = (slab * cos + pltpu.roll(slab, ROPE_DIM // 2, 1) * sin_up
                            + pltpu.roll(slab, LANE - ROPE_DIM // 2, 1) * sin_dn)
                slab = slab.astype(BF16)
                if r == 1:
                    out_ref[s * DIL_HEADS + hh] = slab
                else:
                    for p in range(r):
                        out_ref[s * DIL_HEADS + hh, :, p * hd:(p + 1) * hd] = (
                            slab[p * rows_p:(p + 1) * rows_p, :])
    c0 = len(DIL_GROUPS) * 3 * hw
    qm_ref[...] = _dot(h, w_ref[:, c0:c0 + MEM_WIDTH]).astype(BF16)
    c0 += MEM_WIDTH
    gate_ref[...] = _dot(h, w_ref[:, c0:]).astype(BF16)


def _rope_tables(T, tm):
    half = ROPE_DIM // 2
    inv = ROPE_THETA ** (-jnp.arange(half, dtype=F32) / half)
    ang = jnp.arange(T).astype(F32)[:, None] * inv[None, :]
    cos, sin = jnp.cos(ang), jnp.sin(ang)
    ones = jnp.ones((T, LANE - ROPE_DIM), F32)
    zeros = jnp.zeros((T, LANE - ROPE_DIM), F32)
    zh = jnp.zeros((T, half), F32)
    base = [jnp.concatenate([cos, cos, ones], axis=-1),
            jnp.concatenate([zh, sin, zeros], axis=-1),
            jnp.concatenate([-sin, zh, zeros], axis=-1)]
    tabs = []
    for _, r in DIL_GROUPS:
        for t in base:
            tabs.append(t.reshape(T // tm, tm // r, r, LANE).transpose(0, 2, 1, 3).reshape(T, LANE))
    return jnp.stack(tabs)


def _inproj_b(x2, norm_w, w, T, tm):
    M, D = x2.shape
    N = w.shape[1]
    hd = DIL_HEAD_DIM
    heads_per_group = 3 * DIL_HEADS
    n_tabs = 3 * len(DIL_GROUPS)
    gate_w = N - len(DIL_GROUPS) * heads_per_group * hd - MEM_WIDTH
    tpb = T // tm
    qkv_specs = [pl.BlockSpec((heads_per_group, tm // r, r * hd), lambda i: (0, i, 0))
                 for _, r in DIL_GROUPS]
    qkv_shapes = [jax.ShapeDtypeStruct((heads_per_group, M // r, r * hd), BF16)
                  for _, r in DIL_GROUPS]
    return pl.pallas_call(
        _inproj_b_kernel,
        grid=(M // tm,),
        in_specs=[
            pl.BlockSpec((tm, D), lambda i: (i, 0)),
            pl.BlockSpec((1, D), lambda i: (0, 0)),
            pl.BlockSpec((D, N), lambda i: (0, 0), pipeline_mode=pl.Buffered(1)),
            pl.BlockSpec((n_tabs, tm, LANE), lambda i: (0, i % tpb, 0)),
        ],
        out_specs=qkv_specs + [
            pl.BlockSpec((tm, MEM_WIDTH), lambda i: (i, 0)),
            pl.BlockSpec((tm, gate_w), lambda i: (i, 0)),
        ],
        out_shape=qkv_shapes + [
            jax.ShapeDtypeStruct((M, MEM_WIDTH), BF16),
            jax.ShapeDtypeStruct((M, gate_w), BF16),
        ],
        scratch_shapes=[pltpu.VMEM((D // LANE, tm, LANE), F32)],
        compiler_params=_params(1),
        name="inproj_dil",
    )(x2, norm_w.reshape(1, D), w, _rope_tables(T, tm))


def _gla_kernel(q_ref, k_ref, v_ref, gl_ref, gate_ref, wgu_ref, bg_ref, nw_ref, o_ref):
    T = q_ref.shape[1]
    C = GLA_CHUNK
    U = GLA_UNROLL
    R = U * C
    row = lax.broadcasted_iota(jnp.int32, (R, R), 0)
    col = lax.broadcasted_iota(jnp.int32, (R, R), 1)
    shift = C.bit_length() - 1
    in_chunk_causal = ((row >> shift) == (col >> shift)) & (row >= col)
    tri = jnp.where(in_chunk_causal, 1.0, 0.0).astype(BF16)

    def body(i, st):
        rows = pl.ds(pl.multiple_of(i * R, R), R)
        z = _dot(gl_ref[0, rows, :], wgu_ref[...]) + bg_ref[...]
        la = (jnp.minimum(z, 0.0) - jnp.log(1.0 + jnp.exp(-jnp.abs(z)))) * (1.0 / GLA_TAU)
        hi = la.astype(BF16)
        rem = la - hi.astype(F32)
        mid = rem.astype(BF16)
        lo = (rem - mid.astype(F32)).astype(BF16)
        b = _dot(tri, hi) + _dot(tri, mid) + _dot(tri, lo)
        b_last = [b[c * C + C - 1:(c + 1) * C, :] for c in range(U)]
        bl_rows = jnp.concatenate([jnp.broadcast_to(bl, (C, bl.shape[-1])) for bl in b_last], axis=0)
        qc = q_ref[0, rows, :].astype(F32) * (GLA_DK ** -0.5)
        kc = k_ref[0, rows, :].astype(F32)
        vc = v_ref[0, rows, :]
        q_in = (qc * jnp.exp(b)).astype(BF16)
        k_in = (kc * jnp.exp(-b)).astype(BF16)
        k_out = (kc * jnp.exp(bl_rows - b)).astype(BF16)
        a = jnp.where(in_chunk_causal, _dot_nt(q_in, k_in), 0.0).astype(BF16)
        o_intra = _dot(a, vc)
        o_inter = []
        for c in range(U):
            sl = slice(c * C, (c + 1) * C)
            o_inter.append(_dot_nt(q_in[sl], st.astype(BF16)))
            st = st * jnp.exp(b_last[c]) + _dot_tn(vc[sl], k_out[sl])
        o = o_intra + jnp.concatenate(o_inter, axis=0)
        ms = jnp.sum(o * o, axis=-1, keepdims=True) * (1.0 / GLA_DV)
        y = o * lax.rsqrt(ms + NORM_EPS) * nw_ref[...]
        g = gate_ref[0, rows, :].astype(F32)
        o_ref[0, rows, :] = (y * _silu(g)).astype(BF16)
        return st

    lax.fori_loop(0, T // R, body, jnp.zeros((v_ref.shape[-1], q_ref.shape[-1]), F32), unroll=True)


def _gla(q, k, v, glow, gate, wgu, bg, nw, B, T):
    kw, vw = GLA_DK_PAD, GLA_DV_PAD
    q3 = q.reshape(B, T, GLA_HEADS * kw)
    k3 = k.reshape(B, T, GLA_HEADS * kw)
    v3 = v.reshape(B, T, GLA_HEADS * vw)
    g3 = glow.reshape(B, T, LANE)
    gate3 = gate.reshape(B, T, gate.shape[-1])
    return pl.pallas_call(
        _gla_kernel,
        grid=(B, GLA_HEADS),
        in_specs=[
            pl.BlockSpec((1, T, kw), lambda b, h: (b, 0, h)),
            pl.BlockSpec((1, T, kw), lambda b, h: (b, 0, h)),
            pl.BlockSpec((1, T, vw), lambda b, h: (b, 0, h)),
            pl.BlockSpec((1, T, LANE), lambda b, h: (b, 0, 0)),
            pl.BlockSpec((1, T, vw), lambda b, h: (b, 0, h)),
            pl.BlockSpec((LANE, kw), lambda b, h: (0, h)),
            pl.BlockSpec((1, kw), lambda b, h: (0, h)),
            pl.BlockSpec((1, vw), lambda b, h: (0, 0)),
        ],
        out_specs=pl.BlockSpec((1, T, vw), lambda b, h: (b, 0, h)),
        out_shape=jax.ShapeDtypeStruct((B, T, GLA_HEADS * vw), BF16),
        compiler_params=_params(2),
        name="gla",
    )(q3, k3, v3, g3, gate3, wgu, bg, nw)


def _dil_kernel(q0, k0, v0, q1, k1, v1, q2, k2, v2, gate_ref, o_ref, o_scr, l_scr):
    T = o_ref.shape[1]
    n = DIL_BLOCK
    hd = DIL_HEAD_DIM
    scale = hd ** -0.5
    qi = lax.broadcasted_iota(jnp.int32, (n, 2 * n), 0)
    ki = lax.broadcasted_iota(jnp.int32, (n, 2 * n), 1)
    band = (ki >= qi) & (ki <= qi + n)
    causal = (lax.broadcasted_iota(jnp.int32, (n, n), 1)
              <= lax.broadcasted_iota(jnp.int32, (n, n), 0))
    groups = ((q0, k0, v0), (q1, k1, v1), (q2, k2, v2))

    def run(tasks):
        loaded = []
        for g, p, row0, first in tasks:
            qr, kr, vr = groups[g]
            cols = slice(p * hd, (p + 1) * hd)
            if first:
                win = pl.ds(0, n)
            else:
                win = pl.ds(pl.multiple_of(row0 - n, n), 2 * n)
            loaded.append((qr[0, 0, pl.ds(row0, n), cols], kr[0, 0, win, cols], vr[0, 0, win, cols],
                           causal if first else band))
        s = [jnp.where(mask, _dot_nt(q, kw), -jnp.inf) for q, kw, _, mask in loaded]
        m = [jnp.max(x, axis=-1, keepdims=True) for x in s]
        e = [jnp.exp2((x - mx) * (scale * LOG2_E)) for x, mx in zip(s, m)]
        den = [jnp.sum(x, axis=-1, keepdims=True) for x in e]
        o = [_dot(x.astype(BF16), ld[2]) * (1.0 / d) for x, ld, d in zip(e, loaded, den)]
        lse = [mx * scale + jnp.log(d) for mx, d in zip(m, den)]
        for (g, p, row0, _), ov, lv in zip(tasks, o, lse):
            r = DIL_GROUPS[g][1]
            dst = pl.ds(row0, n) if r == 1 else pl.ds(row0 * r + p, n, stride=r)
            o_scr[g, dst, :] = ov
            l_scr[g, dst, :] = jnp.broadcast_to(lv, ov.shape)

    for g, (_, r) in enumerate(DIL_GROUPS):
        nb = T // r // n
        if nb == 1:
            for p0 in range(0, r, DIL_BATCH):
                run([(g, p, 0, True) for p in range(p0, min(p0 + DIL_BATCH, r))])
        elif r == 1:
            run([(g, 0, 0, True)])
            per = 3
            assert (nb - 1) % per == 0

            def blocks(i, carry, g=g, per=per):
                j0 = 1 + i * per
                run([(g, 0, pl.multiple_of((j0 + d) * n, n), False) for d in range(per)])
                return carry

            lax.fori_loop(0, (nb - 1) // per, blocks, 0, unroll=True)
        else:
            assert r <= DIL_BATCH
            run([(g, p, 0, True) for p in range(r)])

            def phases(j, carry, g=g, r=r):
                row0 = pl.multiple_of(j * n, n)
                run([(g, p, row0, False) for p in range(r)])
                return carry

            lax.fori_loop(1, nb, phases, 0, unroll=True)

    rc = 256

    def combine(c, carry):
        rows = pl.ds(pl.multiple_of(c * rc, rc), rc)
        l0, l1, l2 = l_scr[0, rows, :], l_scr[1, rows, :], l_scr[2, rows, :]
        mx = jnp.maximum(jnp.maximum(l0, l1), l2)
        e0, e1, e2 = jnp.exp(l0 - mx), jnp.exp(l1 - mx), jnp.exp(l2 - mx)
        num = e0 * o_scr[0, rows, :] + e1 * o_scr[1, rows, :] + e2 * o_scr[2, rows, :]
        o = num / (e0 + e1 + e2)
        gt = gate_ref[0, rows, :].astype(F32)
        o_ref[0, rows, :] = (o * _silu(gt)).astype(BF16)
        return carry

    lax.fori_loop(0, T // rc, combine, 0)


def _dilated(qkv_groups, gate, B, T):
    hd = DIL_HEAD_DIM
    in_specs, args = [], []
    for (_, r), qkv in zip(DIL_GROUPS, qkv_groups):
        view = qkv.reshape(qkv.shape[0], B, T // r, r * hd)
        for s in range(3):
            in_specs.append(pl.BlockSpec((1, 1, T // r, r * hd),
                                         lambda b, h, s=s: (s * DIL_HEADS + h, b, 0, 0)))
            args.append(view)
    in_specs.append(pl.BlockSpec((1, T, hd), lambda b, h: (b, 0, h)))
    args.append(gate.reshape(B, T, gate.shape[-1]))
    n_groups = len(DIL_GROUPS)
    return pl.pallas_call(
        _dil_kernel,
        grid=(B, DIL_HEADS),
        in_specs=in_specs,
        out_specs=pl.BlockSpec((1, T, hd), lambda b, h: (b, 0, h)),
        out_shape=jax.ShapeDtypeStruct((B, T, DIL_HEADS * hd), BF16),
        scratch_shapes=[pltpu.VMEM((n_groups, T, hd), F32), pltpu.VMEM((n_groups, T, hd), F32)],
        compiler_params=_params(2),
        name="dilated",
    )(*args)


def _out_kernel(mix_ref, qm_ref, gm_ref, kst_ref, vst_ref, w_ref, x_ref, fnw_ref, o_ref, *, final):
    n_mem = kst_ref.shape[0] // MEM_HEADS
    s = _dot_nt(qm_ref[...], kst_ref[...])
    mo = None
    for hh in range(MEM_HEADS):
        seg = s[:, hh * n_mem:(hh + 1) * n_mem]
        m = jnp.max(seg, axis=-1, keepdims=True)
        e = jnp.exp(seg - m)
        p = (e / jnp.sum(e, axis=-1, keepdims=True)).astype(BF16)
        part = _dot(p, vst_ref[hh * n_mem:(hh + 1) * n_mem, :])
        mo = part if mo is None else mo + part
    bm = (mo * _silu(gm_ref[...].astype(F32))).astype(BF16)
    wm = mix_ref.shape[1]
    if wm % OUT_SPLIT == 0:
        y = _dot(mix_ref[...], w_ref[:wm, :]) + _dot(bm, w_ref[wm:, :]) + x_ref[...]
    else:
        y = _dot(jnp.concatenate([mix_ref[...], bm], axis=-1), w_ref[...]) + x_ref[...]
    if final:
        y = _rmsnorm(y, fnw_ref[...])
    o_ref[...] = y


def _out_proj(mix2, qm2, gate2, kv, layer, w_branch, x2, fnw, T, tm, final):
    M, D = x2.shape
    wm = mix2.shape[1]
    gate_blk = (gate2.shape[1] - MEM_WIDTH) // MEM_WIDTH
    tpb = T // tm
    rows_kv = kv.shape[3]
    kv_spec = lambda which: pl.BlockSpec(
        (None, None, None, rows_kv, MEM_WIDTH), lambda i: (layer, which, i // tpb, 0, 0))
    return pl.pallas_call(
        functools.partial(_out_kernel, final=final),
        grid=(M // tm,),
        in_specs=[
            pl.BlockSpec((tm, wm), lambda i: (i, 0)),
            pl.BlockSpec((tm, MEM_WIDTH), lambda i: (i, 0)),
            pl.BlockSpec((tm, MEM_WIDTH), lambda i: (i, gate_blk)),
            kv_spec(0), kv_spec(1),
            pl.BlockSpec((wm + MEM_WIDTH, D), lambda i: (0, 0)),
            pl.BlockSpec((tm, D), lambda i: (i, 0)),
            pl.BlockSpec((1, D), lambda i: (0, 0)),
        ],
        out_specs=pl.BlockSpec((tm, D), lambda i: (i, 0)),
        out_shape=jax.ShapeDtypeStruct((M, D), F32),
        compiler_params=_params(1),
        name="memattn_outproj",
    )(mix2, qm2, gate2, kv, kv, w_branch, x2, fnw.reshape(1, D))


def _pad_heads(w, heads, width, padded, axis=-1):
    axis = axis % w.ndim
    shape = w.shape[:axis] + (heads, width) + w.shape[axis + 1:]
    pads = [(0, 0)] * (w.ndim + 1)
    pads[axis + 1] = (0, padded - width)
    out = jnp.pad(w.reshape(shape), pads)
    return out.reshape(w.shape[:axis] + (heads * padded,) + w.shape[axis + 1:])


def _gla_weights(w_in, w_gate_up, b_gate, gla_norm_w, w_out):
    hk = GLA_HEADS * GLA_DK
    mixw = GLA_HEADS * GLA_DV
    c = [0, hk, 2 * hk, 2 * hk + mixw, 2 * hk + mixw + GLA_RANK, 2 * hk + mixw + GLA_RANK + MEM_WIDTH]
    q, k, v = w_in[:, c[0]:c[1]], w_in[:, c[1]:c[2]], w_in[:, c[2]:c[3]]
    gl, qm, gate = w_in[:, c[3]:c[4]], w_in[:, c[4]:c[5]], w_in[:, c[5]:]
    w = jnp.concatenate([
        _pad_heads(q, GLA_HEADS, GLA_DK, GLA_DK_PAD),
        _pad_heads(k, GLA_HEADS, GLA_DK, GLA_DK_PAD),
        _pad_heads(v, GLA_HEADS, GLA_DV, GLA_DV_PAD),
        jnp.pad(gl, ((0, 0), (0, LANE - GLA_RANK))),
        qm,
        _pad_heads(gate[:, :mixw], GLA_HEADS, GLA_DV, GLA_DV_PAD),
        gate[:, mixw:],
    ], axis=1).astype(BF16)
    widths = (GLA_HEADS * GLA_DK_PAD, GLA_HEADS * GLA_DK_PAD, GLA_HEADS * GLA_DV_PAD, LANE,
              MEM_WIDTH, GLA_HEADS * GLA_DV_PAD + MEM_WIDTH)
    wgu = jnp.pad(_pad_heads(w_gate_up, GLA_HEADS, GLA_DK, GLA_DK_PAD),
                  ((0, LANE - GLA_RANK), (0, 0))).astype(BF16)
    bg = _pad_heads(b_gate.reshape(1, hk), GLA_HEADS, GLA_DK, GLA_DK_PAD)
    nw = jnp.pad(gla_norm_w.reshape(1, GLA_DV), ((0, 0), (0, GLA_DV_PAD - GLA_DV)))
    w_branch = jnp.concatenate(
        [_pad_heads(w_out[:mixw], GLA_HEADS, GLA_DV, GLA_DV_PAD, axis=0), w_out[mixw:]],
        axis=0).astype(BF16)
    return w, widths, wgu, bg, nw, w_branch


def kernel(x, mem, mem_norm_w, norm_w, w_memkv, w_out, w_in_a, w_gate_up, b_gate, gla_norm_w,
           w_in_b, final_norm_w):
    B, T, D = x.shape
    depth = norm_w.shape[0]
    M = B * T
    tm = 512
    kv = _memkv(mem, mem_norm_w, w_memkv)
    x2 = x.reshape(M, D)
    for i in range(depth):
        j = i // 2
        final = i == depth - 1
        if i % 2 == 0:
            w, widths, wgu, bg, nw, w_branch = _gla_weights(
                w_in_a[j], w_gate_up[j], b_gate[j], gla_norm_w[j], w_out[i])
            q, k, v, glow, qm, gate = _inproj_a(x2, norm_w[i], w, widths, tm)
            mix = _gla(q, k, v, glow, gate, wgu, bg, nw, B, T)
        else:
            w_branch = w_out[i].astype(BF16)
            *qkv_groups, qm, gate = _inproj_b(x2, norm_w[i], w_in_b[j].astype(BF16), T, tm)
            mix = _dilated(qkv_groups, gate, B, T)
        x2 = _out_proj(mix.reshape(M, -1), qm, gate, kv, i, w_branch, x2, final_norm_w, T, tm, final)
    return x2.reshape(B, T, D)
```

```python
import functools

import jax
import jax.numpy as jnp
from jax import lax
from jax.experimental import pallas as pl
from jax.experimental.pallas import tpu as pltpu

F32 = jnp.float32
BF16 = jnp.bfloat16

LANE = 128
NORM_EPS = 1e-6
LOG2_E = 1.4426950408889634
MEM_HEADS = 4
MEM_HEAD_DIM = 64
MEM_WIDTH = MEM_HEADS * MEM_HEAD_DIM
GLA_HEADS = 4
GLA_DK = 96
GLA_DV = 192
GLA_DK_PAD = 128
GLA_DV_PAD = 256
GLA_RANK = 16
GLA_TAU = 16.0
GLA_CHUNK = 64
GLA_UNROLL = 4
DIL_GROUPS = ((128, 1), (512, 4), (2048, 16))
DIL_HEADS = 6
DIL_HEAD_DIM = 128
DIL_BLOCK = 128
DIL_BATCH = 2
OUT_SPLIT = 512
ROPE_THETA = 500000.0
ROPE_DIM = 32
VMEM_LIMIT = 56 * 1024 * 1024

_NT = (((1,), (1,)), ((), ()))
_TN = (((0,), (0,)), ((), ()))


def _dot(a, b):
    return jnp.dot(a, b, preferred_element_type=F32)


def _dot_nt(a, b):
    return lax.dot_general(a, b, _NT, preferred_element_type=F32)


def _dot_tn(a, b):
    return lax.dot_general(a, b, _TN, preferred_element_type=F32)


def _rmsnorm(x, w):
    ms = jnp.mean(x * x, axis=-1, keepdims=True)
    return x * lax.rsqrt(ms + NORM_EPS) * w


def _silu(g):
    return g * (0.5 + 0.5 * jnp.tanh(0.5 * g))


def _software_pipeline(n_items, stages):
    carries = [None] * n_items
    for t in range(n_items + len(stages) - 1):
        for k, stage in enumerate(stages):
            i = t - k
            if 0 <= i < n_items:
                carries[i] = stage(i, carries[i])


def _params(n_parallel):
    return pltpu.CompilerParams(
        dimension_semantics=("parallel",) * n_parallel,
        vmem_limit_bytes=VMEM_LIMIT)


def _memkv_kernel(mem_ref, nw_ref, w_ref, kv_ref, *, depth):
    h = _rmsnorm(mem_ref[0], nw_ref[...]).astype(BF16)
    kv = _dot(h, w_ref[...])
    n_mem = kv.shape[0]
    lane = lax.broadcasted_iota(jnp.int32, (n_mem, MEM_WIDTH), 1)
    for l in range(depth):
        base = l * 2 * MEM_WIDTH
        k = kv[:, base:base + MEM_WIDTH] * (MEM_HEAD_DIM ** -0.5)
        v = kv[:, base + MEM_WIDTH:base + 2 * MEM_WIDTH]
        for hh in range(MEM_HEADS):
            msk = (lane >= hh * MEM_HEAD_DIM) & (lane < (hh + 1) * MEM_HEAD_DIM)
            rows = slice(hh * n_mem, (hh + 1) * n_mem)
            kv_ref[l, 0, 0, rows, :] = jnp.where(msk, k, 0.0).astype(BF16)
            kv_ref[l, 1, 0, rows, :] = jnp.where(msk, v, 0.0).astype(BF16)


def _memkv(mem, mem_norm_w, w_memkv):
    B, n_mem, D = mem.shape
    depth = w_memkv.shape[0]
    w = jnp.transpose(w_memkv, (1, 0, 2)).reshape(D, depth * 2 * MEM_WIDTH).astype(BF16)
    return pl.pallas_call(
        functools.partial(_memkv_kernel, depth=depth),
        grid=(B,),
        in_specs=[
            pl.BlockSpec((1, n_mem, D), lambda b: (b, 0, 0)),
            pl.BlockSpec((1, D), lambda b: (0, 0)),
            pl.BlockSpec((D, depth * 2 * MEM_WIDTH), lambda b: (0, 0)),
        ],
        out_specs=pl.BlockSpec((depth, 2, 1, MEM_HEADS * n_mem, MEM_WIDTH),
                               lambda b: (0, 0, b, 0, 0)),
        out_shape=jax.ShapeDtypeStruct((depth, 2, B, MEM_HEADS * n_mem, MEM_WIDTH), BF16),
        compiler_params=_params(1),
        name="memkv",
    )(mem, mem_norm_w.reshape(1, D), w)


def _inproj_a_kernel(x_ref, nw_ref, w_ref, *out_refs):
    h = _rmsnorm(x_ref[...], nw_ref[...]).astype(BF16)
    c0 = 0
    for o_ref in out_refs:
        cw = o_ref.shape[-1]
        o_ref[...] = _dot(h, w_ref[:, c0:c0 + cw]).astype(BF16)
        c0 += cw


def _inproj_a(x2, norm_w, w, widths, tm):
    M, D = x2.shape
    N = w.shape[1]
    return pl.pallas_call(
        _inproj_a_kernel,
        grid=(M // tm,),
        in_specs=[
            pl.BlockSpec((tm, D), lambda i: (i, 0)),
            pl.BlockSpec((1, D), lambda i: (0, 0)),
            pl.BlockSpec((D, N), lambda i: (0, 0)),
        ],
        out_specs=[pl.BlockSpec((tm, cw), lambda i: (i, 0)) for cw in widths],
        out_shape=[jax.ShapeDtypeStruct((M, cw), BF16) for cw in widths],
        compiler_params=_params(1),
        name="inproj_gla",
    )(x2, norm_w.reshape(1, D), w)


def _inproj_b_kernel(x_ref, nw_ref, w_ref, rope_ref, qkv0_ref, qkv1_ref, qkv2_ref, qm_ref, gate_ref,
                     h_scr):
    hf = _rmsnorm(x_ref[...], nw_ref[...])
    h = hf.astype(BF16)
    tm, D = hf.shape
    hd = DIL_HEAD_DIM
    hw = DIL_HEADS * hd
    n_lt = D // LANE
    for lt in range(n_lt):
        h_scr[lt] = hf[:, lt * LANE:(lt + 1) * LANE]
    group_refs = (qkv0_ref, qkv1_ref, qkv2_ref)
    for g, (_, r) in enumerate(DIL_GROUPS):
        out_ref = group_refs[g]
        rows_p = tm // r
        if r == 1:
            hg = h
        else:
            hg = jnp.concatenate(
                [jnp.concatenate([h_scr[lt, pl.ds(p, rows_p, stride=r), :] for lt in range(n_lt)],
                                 axis=1).astype(BF16) for p in range(r)], axis=0)
        cos, sin_up, sin_dn = rope_ref[3 * g], rope_ref[3 * g + 1], rope_ref[3 * g + 2]
        for s in range(3):
            ci = g * 3 + s
            acc = _dot(hg, w_ref[:, ci * hw:(ci + 1) * hw])
            for hh in range(DIL_HEADS):
                slab = acc[:, hh * hd:(hh + 1) * hd]
                if s < 2:
                    slab = (slab * cos + pltpu.roll(slab, ROPE_DIM // 2, 1) * sin_up
                            + pltpu.roll(slab, LANE - ROPE_DIM // 2, 1) * sin_dn)
                slab = slab.astype(BF16)
                if r == 1:
                    out_ref[s * DIL_HEADS + hh] = slab
                else:
                    for p in range(r):
                        out_ref[s * DIL_HEADS + hh, :, p * hd:(p + 1) * hd] = (
                            slab[p * rows_p:(p + 1) * rows_p, :])
    c0 = len(DIL_GROUPS) * 3 * hw
    qm_ref[...] = _dot(h, w_ref[:, c0:c0 + MEM_WIDTH]).astype(BF16)
    c0 += MEM_WIDTH
    gate_ref[...] = _dot(h, w_ref[:, c0:]).astype(BF16)


def _rope_tables(T, tm):
    half = ROPE_DIM // 2
    inv = ROPE_THETA ** (-jnp.arange(half, dtype=F32) / half)
    ang = jnp.arange(T).astype(F32)[:, None] * inv[None, :]
    cos, sin = jnp.cos(ang), jnp.sin(ang)
    ones = jnp.ones((T, LANE - ROPE_DIM), F32)
    zeros = jnp.zeros((T, LANE - ROPE_DIM), F32)
    zh = jnp.zeros((T, half), F32)
    base = [jnp.concatenate([cos, cos, ones], axis=-1),
            jnp.concatenate([zh, sin, zeros], axis=-1),
            jnp.concatenate([-sin, zh, zeros], axis=-1)]
    tabs = []
    for _, r in DIL_GROUPS:
        for t in base:
            tabs.append(t.reshape(T // tm, tm // r, r, LANE).transpose(0, 2, 1, 3).reshape(T, LANE))
    return jnp.stack(tabs)


def _inproj_b(x2, norm_w, w, T, tm):
    M, D = x2.shape
    N = w.shape[1]
    hd = DIL_HEAD_DIM
    heads_per_group = 3 * DIL_HEADS
    n_tabs = 3 * len(DIL_GROUPS)
    gate_w = N - len(DIL_GROUPS) * heads_per_group * hd - MEM_WIDTH
    tpb = T // tm
    qkv_specs = [pl.BlockSpec((heads_per_group, tm // r, r * hd), lambda i: (0, i, 0))
                 for _, r in DIL_GROUPS]
    qkv_shapes = [jax.ShapeDtypeStruct((heads_per_group, M // r, r * hd), BF16)
                  for _, r in DIL_GROUPS]
    return pl.pallas_call(
        _inproj_b_kernel,
        grid=(M // tm,),
        in_specs=[
            pl.BlockSpec((tm, D), lambda i: (i, 0)),
            pl.BlockSpec((1, D), lambda i: (0, 0)),
            pl.BlockSpec((D, N), lambda i: (0, 0), pipeline_mode=pl.Buffered(1)),
            pl.BlockSpec((n_tabs, tm, LANE), lambda i: (0, i % tpb, 0)),
        ],
        out_specs=qkv_specs + [
            pl.BlockSpec((tm, MEM_WIDTH), lambda i: (i, 0)),
            pl.BlockSpec((tm, gate_w), lambda i: (i, 0)),
        ],
        out_shape=qkv_shapes + [
            jax.ShapeDtypeStruct((M, MEM_WIDTH), BF16),
            jax.ShapeDtypeStruct((M, gate_w), BF16),
        ],
        scratch_shapes=[pltpu.VMEM((D // LANE, tm, LANE), F32)],
        compiler_params=_params(1),
        name="inproj_dil",
    )(x2, norm_w.reshape(1, D), w, _rope_tables(T, tm))


def _gla_kernel(q_ref, k_ref, v_ref, gl_ref, gate_ref, wgu_ref, bg_ref, nw_ref, o_ref):
    T = q_ref.shape[1]
    C = GLA_CHUNK
    U = GLA_UNROLL
    R = U * C
    row = lax.broadcasted_iota(jnp.int32, (R, R), 0)
    col = lax.broadcasted_iota(jnp.int32, (R, R), 1)
    shift = C.bit_length() - 1
    in_chunk_causal = ((row >> shift) == (col >> shift)) & (row >= col)
    tri = jnp.where(in_chunk_causal, 1.0, 0.0).astype(BF16)
    state = [jnp.zeros((v_ref.shape[-1], q_ref.shape[-1]), F32)]

    def gate_logits(i, _):
        rows = slice(i * R, (i + 1) * R)
        z = _dot(gl_ref[0, rows, :], wgu_ref[...]) + bg_ref[...]
        la = (jnp.minimum(z, 0.0) - jnp.log(1.0 + jnp.exp(-jnp.abs(z)))) * (1.0 / GLA_TAU)
        hi = la.astype(BF16)
        rem = la - hi.astype(F32)
        mid = rem.astype(BF16)
        lo = (rem - mid.astype(F32)).astype(BF16)
        return rows, hi, mid, lo

    def decays(i, carry):
        rows, hi, mid, lo = carry
        b = _dot(tri, hi) + _dot(tri, mid) + _dot(tri, lo)
        b_last = [b[c * C + C - 1:(c + 1) * C, :] for c in range(U)]
        bl_rows = jnp.concatenate([jnp.broadcast_to(bl, (C, bl.shape[-1])) for bl in b_last], axis=0)
        qc = q_ref[0, rows, :].astype(F32) * (GLA_DK ** -0.5)
        kc = k_ref[0, rows, :].astype(F32)
        q_in = (qc * jnp.exp(b)).astype(BF16)
        k_in = (kc * jnp.exp(-b)).astype(BF16)
        k_out = (kc * jnp.exp(bl_rows - b)).astype(BF16)
        return rows, q_in, k_in, k_out, [jnp.exp(bl) for bl in b_last]

    def intra(i, carry):
        rows, q_in, k_in, k_out, dec = carry
        vc = v_ref[0, rows, :]
        a = _dot_nt(q_in, k_in)
        ds = [_dot_tn(vc[c * C:(c + 1) * C], k_out[c * C:(c + 1) * C]) for c in range(U)]
        o_intra = _dot(jnp.where(in_chunk_causal, a, 0.0).astype(BF16), vc)
        return rows, q_in, o_intra, ds, dec

    def inter(i, carry):
        rows, q_in, o_intra, ds, dec = carry
        st = state[0]
        o_inter = []
        for c in range(U):
            o_inter.append(_dot_nt(q_in[c * C:(c + 1) * C], st.astype(BF16)))
            st = st * dec[c] + ds[c]
        state[0] = st
        o = o_intra + jnp.concatenate(o_inter, axis=0)
        ms = jnp.sum(o * o, axis=-1, keepdims=True) * (1.0 / GLA_DV)
        y = o * lax.rsqrt(ms + NORM_EPS) * nw_ref[...]
        g = gate_ref[0, rows, :].astype(F32)
        o_ref[0, rows, :] = (y * _silu(g)).astype(BF16)
        return None

    _software_pipeline(T // R, (gate_logits, decays, intra, inter))


def _gla(q, k, v, glow, gate, wgu, bg, nw, B, T):
    kw, vw = GLA_DK_PAD, GLA_DV_PAD
    q3 = q.reshape(B, T, GLA_HEADS * kw)
    k3 = k.reshape(B, T, GLA_HEADS * kw)
    v3 = v.reshape(B, T, GLA_HEADS * vw)
    g3 = glow.reshape(B, T, LANE)
    gate3 = gate.reshape(B, T, gate.shape[-1])
    return pl.pallas_call(
        _gla_kernel,
        grid=(B, GLA_HEADS),
        in_specs=[
            pl.BlockSpec((1, T, kw), lambda b, h: (b, 0, h)),
            pl.BlockSpec((1, T, kw), lambda b, h: (b, 0, h)),
            pl.BlockSpec((1, T, vw), lambda b, h: (b, 0, h)),
            pl.BlockSpec((1, T, LANE), lambda b, h: (b, 0, 0)),
            pl.BlockSpec((1, T, vw), lambda b, h: (b, 0, h)),
            pl.BlockSpec((LANE, kw), lambda b, h: (0, h)),
            pl.BlockSpec((1, kw), lambda b, h: (0, h)),
            pl.BlockSpec((1, vw), lambda b, h: (0, 0)),
        ],
        out_specs=pl.BlockSpec((1, T, vw), lambda b, h: (b, 0, h)),
        out_shape=jax.ShapeDtypeStruct((B, T, GLA_HEADS * vw), BF16),
        compiler_params=_params(2),
        name="gla",
    )(q3, k3, v3, g3, gate3, wgu, bg, nw)


def _dil_kernel(q0, k0, v0, q1, k1, v1, q2, k2, v2, gate_ref, o_ref, o_scr, l_scr):
    T = o_ref.shape[1]
    n = DIL_BLOCK
    hd = DIL_HEAD_DIM
    scale = hd ** -0.5
    qi = lax.broadcasted_iota(jnp.int32, (n, 2 * n), 0)
    ki = lax.broadcasted_iota(jnp.int32, (n, 2 * n), 1)
    band = (ki >= qi) & (ki <= qi + n)
    causal = (lax.broadcasted_iota(jnp.int32, (n, n), 1)
              <= lax.broadcasted_iota(jnp.int32, (n, n), 0))
    groups = ((q0, k0, v0), (q1, k1, v1), (q2, k2, v2))

    dilated = [g for g, (_, r) in enumerate(DIL_GROUPS) if r > 1]
    dense = [g for g, (_, r) in enumerate(DIL_GROUPS) if r == 1]
    assert len(dense) == 1 and o_scr.shape[0] == len(dilated)

    tasks = []
    for g in dilated + dense:
        r = DIL_GROUPS[g][1]
        tasks += [(g, p, j) for j in range(T // r // n) for p in range(r)]
    batches = [tasks[t:t + DIL_BATCH] for t in range(0, len(tasks), DIL_BATCH)]

    def scores(i, _):
        out = []
        for g, p, j in batches[i]:
            qr, kr, vr = groups[g]
            cols = slice(p * hd, (p + 1) * hd)
            win = slice(0, n) if j == 0 else slice((j - 1) * n, (j + 1) * n)
            s = _dot_nt(qr[0, 0, j * n:(j + 1) * n, cols], kr[0, 0, win, cols])
            out.append((jnp.where(causal if j == 0 else band, s, -jnp.inf), vr[0, 0, win, cols]))
        return out

    def row_max(i, carry):
        return [(s, vw, jnp.max(s, axis=-1, keepdims=True)) for s, vw in carry]

    def exponentials(i, carry):
        out = []
        for s, vw, m in carry:
            e = jnp.exp2((s - m) * (scale * LOG2_E))
            out.append((e.astype(BF16), vw, m, jnp.sum(e, axis=-1, keepdims=True)))
        return out

    def weighted_values(i, carry):
        for (g, p, j), (e, vw, m, den) in zip(batches[i], carry):
            o = _dot(e, vw) * (1.0 / den)
            lse = jnp.broadcast_to(m * scale + jnp.log(den), o.shape)
            r = DIL_GROUPS[g][1]
            if r > 1:
                slot = dilated.index(g)
                dst = pl.ds(j * n * r + p, n, stride=r)
                o_scr[slot, dst, :] = o
                l_scr[slot, dst, :] = lse
            else:
                rows = slice(j * n, (j + 1) * n)
                ls = [lse] + [l_scr[slot, rows, :] for slot in range(len(dilated))]
                os_ = [o] + [o_scr[slot, rows, :] for slot in range(len(dilated))]
                mx = functools.reduce(jnp.maximum, ls)
                ws = [jnp.exp(l - mx) for l in ls]
                num = functools.reduce(lambda a, b: a + b, [w * ov for w, ov in zip(ws, os_)])
                mixed = num / functools.reduce(lambda a, b: a + b, ws)
                gt = gate_ref[0, rows, :].astype(F32)
                o_ref[0, rows, :] = (mixed * _silu(gt)).astype(BF16)
        return None

    _software_pipeline(len(batches), (scores, row_max, exponentials, weighted_values))


def _dilated(qkv_groups, gate, B, T):
    hd = DIL_HEAD_DIM
    in_specs, args = [], []
    for (_, r), qkv in zip(DIL_GROUPS, qkv_groups):
        view = qkv.reshape(qkv.shape[0], B, T // r, r * hd)
        for s in range(3):
            in_specs.append(pl.BlockSpec((1, 1, T // r, r * hd),
                                         lambda b, h, s=s: (s * DIL_HEADS + h, b, 0, 0)))
            args.append(view)
    in_specs.append(pl.BlockSpec((1, T, hd), lambda b, h: (b, 0, h)))
    args.append(gate.reshape(B, T, gate.shape[-1]))
    n_groups = sum(r > 1 for _, r in DIL_GROUPS)
    return pl.pallas_call(
        _dil_kernel,
        grid=(B, DIL_HEADS),
        in_specs=in_specs,
        out_specs=pl.BlockSpec((1, T, hd), lambda b, h: (b, 0, h)),
        out_shape=jax.ShapeDtypeStruct((B, T, DIL_HEADS * hd), BF16),
        scratch_shapes=[pltpu.VMEM((n_groups, T, hd), F32), pltpu.VMEM((n_groups, T, hd), F32)],
        compiler_params=_params(2),
        name="dilated",
    )(*args)


def _out_kernel(mix_ref, qm_ref, gm_ref, kst_ref, vst_ref, w_ref, x_ref, fnw_ref, o_ref, *, final):
    n_mem = kst_ref.shape[0] // MEM_HEADS
    wm = mix_ref.shape[1]
    s = _dot_nt(qm_ref[...], kst_ref[...])
    split = wm % OUT_SPLIT == 0
    if split:
        y = _dot(mix_ref[...], w_ref[:wm, :])
    mo = None
    for hh in range(MEM_HEADS):
        seg = s[:, hh * n_mem:(hh + 1) * n_mem]
        m = jnp.max(seg, axis=-1, keepdims=True)
        e = jnp.exp(seg - m)
        p = (e / jnp.sum(e, axis=-1, keepdims=True)).astype(BF16)
        part = _dot(p, vst_ref[hh * n_mem:(hh + 1) * n_mem, :])
        mo = part if mo is None else mo + part
    bm = (mo * _silu(gm_ref[...].astype(F32))).astype(BF16)
    if split:
        y = y + _dot(bm, w_ref[wm:, :]) + x_ref[...]
    else:
        y = _dot(jnp.concatenate([mix_ref[...], bm], axis=-1), w_ref[...]) + x_ref[...]
    if final:
        y = _rmsnorm(y, fnw_ref[...])
    o_ref[...] = y


def _out_proj(mix2, qm2, gate2, kv, layer, w_branch, x2, fnw, T, tm, final):
    M, D = x2.shape
    wm = mix2.shape[1]
    gate_blk = (gate2.shape[1] - MEM_WIDTH) // MEM_WIDTH
    tpb = T // tm
    rows_kv = kv.shape[3]
    kv_spec = lambda which: pl.BlockSpec(
        (None, None, None, rows_kv, MEM_WIDTH), lambda i: (layer, which, i // tpb, 0, 0))
    return pl.pallas_call(
        functools.partial(_out_kernel, final=final),
        grid=(M // tm,),
        in_specs=[
            pl.BlockSpec((tm, wm), lambda i: (i, 0)),
            pl.BlockSpec((tm, MEM_WIDTH), lambda i: (i, 0)),
            pl.BlockSpec((tm, MEM_WIDTH), lambda i: (i, gate_blk)),
            kv_spec(0), kv_spec(1),
            pl.BlockSpec((wm + MEM_WIDTH, D), lambda i: (0, 0)),
            pl.BlockSpec((tm, D), lambda i: (i, 0)),
            pl.BlockSpec((1, D), lambda i: (0, 0)),
        ],
        out_specs=pl.BlockSpec((tm, D), lambda i: (i, 0)),
        out_shape=jax.ShapeDtypeStruct((M, D), F32),
        compiler_params=_params(1),
        name="memattn_outproj",
    )(mix2, qm2, gate2, kv, kv, w_branch, x2, fnw.reshape(1, D))


def _pad_heads(w, heads, width, padded, axis=-1):
    axis = axis % w.ndim
    shape = w.shape[:axis] + (heads, width) + w.shape[axis + 1:]
    pads = [(0, 0)] * (w.ndim + 1)
    pads[axis + 1] = (0, padded - width)
    out = jnp.pad(w.reshape(shape), pads)
    return out.reshape(w.shape[:axis] + (heads * padded,) + w.shape[axis + 1:])


def _gla_weights(w_in, w_gate_up, b_gate, gla_norm_w, w_out):
    hk = GLA_HEADS * GLA_DK
    mixw = GLA_HEADS * GLA_DV
    c = [0, hk, 2 * hk, 2 * hk + mixw, 2 * hk + mixw + GLA_RANK, 2 * hk + mixw + GLA_RANK + MEM_WIDTH]
    q, k, v = w_in[:, c[0]:c[1]], w_in[:, c[1]:c[2]], w_in[:, c[2]:c[3]]
    gl, qm, gate = w_in[:, c[3]:c[4]], w_in[:, c[4]:c[5]], w_in[:, c[5]:]
    w = jnp.concatenate([
        _pad_heads(q, GLA_HEADS, GLA_DK, GLA_DK_PAD),
        _pad_heads(k, GLA_HEADS, GLA_DK, GLA_DK_PAD),
        _pad_heads(v, GLA_HEADS, GLA_DV, GLA_DV_PAD),
        jnp.pad(gl, ((0, 0), (0, LANE - GLA_RANK))),
        qm,
        _pad_heads(gate[:, :mixw], GLA_HEADS, GLA_DV, GLA_DV_PAD),
        gate[:, mixw:],
    ], axis=1).astype(BF16)
    widths = (GLA_HEADS * GLA_DK_PAD, GLA_HEADS * GLA_DK_PAD, GLA_HEADS * GLA_DV_PAD, LANE,
              MEM_WIDTH, GLA_HEADS * GLA_DV_PAD + MEM_WIDTH)
    wgu = jnp.pad(_pad_heads(w_gate_up, GLA_HEADS, GLA_DK, GLA_DK_PAD),
                  ((0, LANE - GLA_RANK), (0, 0))).astype(BF16)
    bg = _pad_heads(b_gate.reshape(1, hk), GLA_HEADS, GLA_DK, GLA_DK_PAD)
    nw = jnp.pad(gla_norm_w.reshape(1, GLA_DV), ((0, 0), (0, GLA_DV_PAD - GLA_DV)))
    w_branch = jnp.concatenate(
        [_pad_heads(w_out[:mixw], GLA_HEADS, GLA_DV, GLA_DV_PAD, axis=0), w_out[mixw:]],
        axis=0).astype(BF16)
    return w, widths, wgu, bg, nw, w_branch


def kernel(x, mem, mem_norm_w, norm_w, w_memkv, w_out, w_in_a, w_gate_up, b_gate, gla_norm_w,
           w_in_b, final_norm_w):
    B, T, D = x.shape
    depth = norm_w.shape[0]
    M = B * T
    tm = 512
    kv = _memkv(mem, mem_norm_w, w_memkv)
    x2 = x.reshape(M, D)
    for i in range(depth):
        j = i // 2
        final = i == depth - 1
        if i % 2 == 0:
            w, widths, wgu, bg, nw, w_branch = _gla_weights(
                w_in_a[j], w_gate_up[j], b_gate[j], gla_norm_w[j], w_out[i])
            q, k, v, glow, qm, gate = _inproj_a(x2, norm_w[i], w, widths, tm)
            mix = _gla(q, k, v, glow, gate, wgu, bg, nw, B, T)
        else:
            w_branch = w_out[i].astype(BF16)
            *qkv_groups, qm, gate = _inproj_b(x2, norm_w[i], w_in_b[j].astype(BF16), T, tm)
            mix = _dilated(qkv_groups, gate, B, T)
        x2 = _out_proj(mix.reshape(M, -1), qm, gate, kv, i, w_branch, x2, final_norm_w, T, tm, final)
    return x2.reshape(B, T, D)
```

```python
import functools

import jax
import jax.numpy as jnp
from jax import lax
from jax.experimental import pallas as pl
from jax.experimental.pallas import tpu as pltpu

F32 = jnp.float32
BF16 = jnp.bfloat16

LANE = 128
NORM_EPS = 1e-6
LOG2_E = 1.4426950408889634
MEM_HEADS = 4
MEM_HEAD_DIM = 64
MEM_WIDTH = MEM_HEADS * MEM_HEAD_DIM
GLA_HEADS = 4
GLA_DK = 96
GLA_DV = 192
GLA_DK_PAD = 128
GLA_DV_PAD = 256
GLA_RANK = 16
GLA_TAU = 16.0
GLA_CHUNK = 64
GLA_UNROLL = 4
DIL_GROUPS = ((128, 1), (512, 4), (2048, 16))
DIL_HEADS = 6
DIL_HEAD_DIM = 128
DIL_BLOCK = 128
DIL_BATCH = 2
OUT_SPLIT = 512
OUT_ROWS = 256
ROPE_THETA = 500000.0
ROPE_DIM = 32
VMEM_LIMIT = 56 * 1024 * 1024

_NT = (((1,), (1,)), ((), ()))
_TN = (((0,), (0,)), ((), ()))


def _dot(a, b):
    return jnp.dot(a, b, preferred_element_type=F32)


def _dot_nt(a, b):
    return lax.dot_general(a, b, _NT, preferred_element_type=F32)


def _dot_tn(a, b):
    return lax.dot_general(a, b, _TN, preferred_element_type=F32)


def _rmsnorm(x, w):
    ms = jnp.mean(x * x, axis=-1, keepdims=True)
    return x * lax.rsqrt(ms + NORM_EPS) * w


def _silu(g):
    return g * (0.5 + 0.5 * jnp.tanh(0.5 * g))


def _software_pipeline(n_items, stages):
    carries = [None] * n_items
    for t in range(n_items + len(stages) - 1):
        for k, stage in enumerate(stages):
            i = t - k
            if 0 <= i < n_items:
                carries[i] = stage(i, carries[i])


def _params(n_parallel):
    return pltpu.CompilerParams(
        dimension_semantics=("parallel",) * n_parallel,
        vmem_limit_bytes=VMEM_LIMIT)


def _memkv_kernel(mem_ref, nw_ref, w_ref, kv_ref, *, depth):
    h = _rmsnorm(mem_ref[0], nw_ref[...]).astype(BF16)
    kv = _dot(h, w_ref[...])
    n_mem = kv.shape[0]
    lane = lax.broadcasted_iota(jnp.int32, (n_mem, MEM_WIDTH), 1)
    for l in range(depth):
        base = l * 2 * MEM_WIDTH
        k = kv[:, base:base + MEM_WIDTH] * (MEM_HEAD_DIM ** -0.5)
        v = kv[:, base + MEM_WIDTH:base + 2 * MEM_WIDTH]
        for hh in range(MEM_HEADS):
            msk = (lane >= hh * MEM_HEAD_DIM) & (lane < (hh + 1) * MEM_HEAD_DIM)
            rows = slice(hh * n_mem, (hh + 1) * n_mem)
            kv_ref[l, 0, 0, rows, :] = jnp.where(msk, k, 0.0).astype(BF16)
            kv_ref[l, 1, 0, rows, :] = jnp.where(msk, v, 0.0).astype(BF16)


def _memkv(mem, mem_norm_w, w_memkv):
    B, n_mem, D = mem.shape
    depth = w_memkv.shape[0]
    w = jnp.transpose(w_memkv, (1, 0, 2)).reshape(D, depth * 2 * MEM_WIDTH).astype(BF16)
    return pl.pallas_call(
        functools.partial(_memkv_kernel, depth=depth),
        grid=(B,),
        in_specs=[
            pl.BlockSpec((1, n_mem, D), lambda b: (b, 0, 0)),
            pl.BlockSpec((1, D), lambda b: (0, 0)),
            pl.BlockSpec((D, depth * 2 * MEM_WIDTH), lambda b: (0, 0)),
        ],
        out_specs=pl.BlockSpec((depth, 2, 1, MEM_HEADS * n_mem, MEM_WIDTH),
                               lambda b: (0, 0, b, 0, 0)),
        out_shape=jax.ShapeDtypeStruct((depth, 2, B, MEM_HEADS * n_mem, MEM_WIDTH), BF16),
        compiler_params=_params(1),
        name="memkv",
    )(mem, mem_norm_w.reshape(1, D), w)


def _inproj_a_kernel(x_ref, nw_ref, w_ref, *out_refs):
    h = _rmsnorm(x_ref[...], nw_ref[...]).astype(BF16)
    c0 = 0
    for o_ref in out_refs:
        cw = o_ref.shape[-1]
        o_ref[...] = _dot(h, w_ref[:, c0:c0 + cw]).astype(BF16)
        c0 += cw


def _inproj_a(x2, norm_w, w, widths, tm):
    M, D = x2.shape
    N = w.shape[1]
    return pl.pallas_call(
        _inproj_a_kernel,
        grid=(M // tm,),
        in_specs=[
            pl.BlockSpec((tm, D), lambda i: (i, 0)),
            pl.BlockSpec((1, D), lambda i: (0, 0)),
            pl.BlockSpec((D, N), lambda i: (0, 0)),
        ],
        out_specs=[pl.BlockSpec((tm, cw), lambda i: (i, 0)) for cw in widths],
        out_shape=[jax.ShapeDtypeStruct((M, cw), BF16) for cw in widths],
        compiler_params=_params(1),
        name="inproj_gla",
    )(x2, norm_w.reshape(1, D), w)


def _inproj_b_kernel(x_ref, nw_ref, w_ref, rope_ref, qkv0_ref, qkv1_ref, qkv2_ref, qm_ref, gate_ref,
                     h_scr):
    hf = _rmsnorm(x_ref[...], nw_ref[...])
    h = hf.astype(BF16)
    tm, D = hf.shape
    hd = DIL_HEAD_DIM
    hw = DIL_HEADS * hd
    n_lt = D // LANE
    for lt in range(n_lt):
        h_scr[lt] = hf[:, lt * LANE:(lt + 1) * LANE]
    group_refs = (qkv0_ref, qkv1_ref, qkv2_ref)
    for g, (_, r) in enumerate(DIL_GROUPS):
        out_ref = group_refs[g]
        rows_p = tm // r
        if r == 1:
            hg = h
        else:
            hg = jnp.concatenate(
                [jnp.concatenate([h_scr[lt, pl.ds(p, rows_p, stride=r), :] for lt in range(n_lt)],
                                 axis=1).astype(BF16) for p in range(r)], axis=0)
        cos, sin_up, sin_dn = rope_ref[3 * g], rope_ref[3 * g + 1], rope_ref[3 * g + 2]
        for s in range(3):
            ci = g * 3 + s
            acc = _dot(hg, w_ref[:, ci * hw:(ci + 1) * hw])
            for hh in range(DIL_HEADS):
                slab = acc[:, hh * hd:(hh + 1) * hd]
                if s < 2:
                    slab = (slab * cos + pltpu.roll(slab, ROPE_DIM // 2, 1) * sin_up
                            + pltpu.roll(slab, LANE - ROPE_DIM // 2, 1) * sin_dn)
                if s == 0:
                    slab = slab * (hd ** -0.5 * LOG2_E)
                slab = slab.astype(BF16)
                if r == 1:
                    out_ref[s * DIL_HEADS + hh] = slab
                else:
                    for p in range(r):
                        out_ref[s * DIL_HEADS + hh, :, p * hd:(p + 1) * hd] = (
                            slab[p * rows_p:(p + 1) * rows_p, :])
    c0 = len(DIL_GROUPS) * 3 * hw
    qm_ref[...] = _dot(h, w_ref[:, c0:c0 + MEM_WIDTH]).astype(BF16)
    c0 += MEM_WIDTH
    gate_ref[...] = _dot(h, w_ref[:, c0:]).astype(BF16)


def _rope_tables(T, tm):
    half = ROPE_DIM // 2
    inv = ROPE_THETA ** (-jnp.arange(half, dtype=F32) / half)
    ang = jnp.arange(T).astype(F32)[:, None] * inv[None, :]
    cos, sin = jnp.cos(ang), jnp.sin(ang)
    ones = jnp.ones((T, LANE - ROPE_DIM), F32)
    zeros = jnp.zeros((T, LANE - ROPE_DIM), F32)
    zh = jnp.zeros((T, half), F32)
    base = [jnp.concatenate([cos, cos, ones], axis=-1),
            jnp.concatenate([zh, sin, zeros], axis=-1),
            jnp.concatenate([-sin, zh, zeros], axis=-1)]
    tabs = []
    for _, r in DIL_GROUPS:
        for t in base:
            tabs.append(t.reshape(T // tm, tm // r, r, LANE).transpose(0, 2, 1, 3).reshape(T, LANE))
    return jnp.stack(tabs)


def _inproj_b(x2, norm_w, w, T, tm):
    M, D = x2.shape
    N = w.shape[1]
    hd = DIL_HEAD_DIM
    heads_per_group = 3 * DIL_HEADS
    n_tabs = 3 * len(DIL_GROUPS)
    gate_w = N - len(DIL_GROUPS) * heads_per_group * hd - MEM_WIDTH
    tpb = T // tm
    qkv_specs = [pl.BlockSpec((heads_per_group, tm // r, r * hd), lambda i: (0, i, 0))
                 for _, r in DIL_GROUPS]
    qkv_shapes = [jax.ShapeDtypeStruct((heads_per_group, M // r, r * hd), BF16)
                  for _, r in DIL_GROUPS]
    return pl.pallas_call(
        _inproj_b_kernel,
        grid=(M // tm,),
        in_specs=[
            pl.BlockSpec((tm, D), lambda i: (i, 0)),
            pl.BlockSpec((1, D), lambda i: (0, 0)),
            pl.BlockSpec((D, N), lambda i: (0, 0), pipeline_mode=pl.Buffered(1)),
            pl.BlockSpec((n_tabs, tm, LANE), lambda i: (0, i % tpb, 0)),
        ],
        out_specs=qkv_specs + [
            pl.BlockSpec((tm, MEM_WIDTH), lambda i: (i, 0)),
            pl.BlockSpec((tm, gate_w), lambda i: (i, 0)),
        ],
        out_shape=qkv_shapes + [
            jax.ShapeDtypeStruct((M, MEM_WIDTH), BF16),
            jax.ShapeDtypeStruct((M, gate_w), BF16),
        ],
        scratch_shapes=[pltpu.VMEM((D // LANE, tm, LANE), F32)],
        compiler_params=_params(1),
        name="inproj_dil",
    )(x2, norm_w.reshape(1, D), w, _rope_tables(T, tm))


def _gla_kernel(q_ref, k_ref, v_ref, gl_ref, gate_ref, wgu_ref, bg_ref, nw_ref, o_ref):
    T = q_ref.shape[1]
    C = GLA_CHUNK
    U = GLA_UNROLL
    R = U * C
    row = lax.broadcasted_iota(jnp.int32, (R, R), 0)
    col = lax.broadcasted_iota(jnp.int32, (R, R), 1)
    shift = C.bit_length() - 1
    in_chunk_causal = ((row >> shift) == (col >> shift)) & (row >= col)
    tri = jnp.where(in_chunk_causal, 1.0, 0.0).astype(BF16)
    state = [jnp.zeros((v_ref.shape[-1], q_ref.shape[-1]), F32)]

    def gate_logits(i, _):
        rows = slice(i * R, (i + 1) * R)
        z = _dot(gl_ref[0, rows, :], wgu_ref[...]) + bg_ref[...]
        la = (jnp.minimum(z, 0.0) * (LOG2_E / GLA_TAU)
              - jnp.log2(1.0 + jnp.exp2(jnp.abs(z) * -LOG2_E)) * (1.0 / GLA_TAU))
        hi = la.astype(BF16)
        rem = la - hi.astype(F32)
        mid = rem.astype(BF16)
        lo = (rem - mid.astype(F32)).astype(BF16)
        return rows, hi, mid, lo

    def decays(i, carry):
        rows, hi, mid, lo = carry
        b = _dot(tri, hi) + _dot(tri, mid) + _dot(tri, lo)
        b_last = [b[c * C + C - 1:(c + 1) * C, :] for c in range(U)]
        bl_rows = jnp.concatenate([jnp.broadcast_to(bl, (C, bl.shape[-1])) for bl in b_last], axis=0)
        qc = q_ref[0, rows, :].astype(F32) * (GLA_DK ** -0.5)
        kc = k_ref[0, rows, :].astype(F32)
        q_in = (qc * jnp.exp2(b)).astype(BF16)
        k_in = (kc * jnp.exp2(-b)).astype(BF16)
        k_out = (kc * jnp.exp2(bl_rows - b)).astype(BF16)
        return rows, q_in, k_in, k_out, [jnp.exp2(bl) for bl in b_last]

    def intra(i, carry):
        rows, q_in, k_in, k_out, dec = carry
        vc = v_ref[0, rows, :]
        a = _dot_nt(q_in, k_in)
        ds = [_dot_tn(vc[c * C:(c + 1) * C], k_out[c * C:(c + 1) * C]) for c in range(U)]
        o_intra = _dot(jnp.where(in_chunk_causal, a, 0.0).astype(BF16), vc)
        return rows, q_in, o_intra, ds, dec

    def inter(i, carry):
        rows, q_in, o_intra, ds, dec = carry
        st = state[0]
        o_inter = []
        for c in range(U):
            o_inter.append(_dot_nt(q_in[c * C:(c + 1) * C], st.astype(BF16)))
            st = st * dec[c] + ds[c]
        state[0] = st
        o = o_intra + jnp.concatenate(o_inter, axis=0)
        ms = jnp.sum(o * o, axis=-1, keepdims=True) * (1.0 / GLA_DV)
        y = o * lax.rsqrt(ms + NORM_EPS) * nw_ref[...]
        g = gate_ref[0, rows, :].astype(F32)
        o_ref[0, rows, :] = (y * _silu(g)).astype(BF16)
        return None

    _software_pipeline(T // R, (gate_logits, decays, intra, inter))


def _gla(q, k, v, glow, gate, wgu, bg, nw, B, T):
    kw, vw = GLA_DK_PAD, GLA_DV_PAD
    q3 = q.reshape(B, T, GLA_HEADS * kw)
    k3 = k.reshape(B, T, GLA_HEADS * kw)
    v3 = v.reshape(B, T, GLA_HEADS * vw)
    g3 = glow.reshape(B, T, LANE)
    gate3 = gate.reshape(B, T, gate.shape[-1])
    return pl.pallas_call(
        _gla_kernel,
        grid=(B, GLA_HEADS),
        in_specs=[
            pl.BlockSpec((1, T, kw), lambda b, h: (b, 0, h)),
            pl.BlockSpec((1, T, kw), lambda b, h: (b, 0, h)),
            pl.BlockSpec((1, T, vw), lambda b, h: (b, 0, h)),
            pl.BlockSpec((1, T, LANE), lambda b, h: (b, 0, 0)),
            pl.BlockSpec((1, T, vw), lambda b, h: (b, 0, h)),
            pl.BlockSpec((LANE, kw), lambda b, h: (0, h)),
            pl.BlockSpec((1, kw), lambda b, h: (0, h)),
            pl.BlockSpec((1, vw), lambda b, h: (0, 0)),
        ],
        out_specs=pl.BlockSpec((1, T, vw), lambda b, h: (b, 0, h)),
        out_shape=jax.ShapeDtypeStruct((B, T, GLA_HEADS * vw), BF16),
        compiler_params=_params(2),
        name="gla",
    )(q3, k3, v3, g3, gate3, wgu, bg, nw)


def _dil_kernel(q0, k0, v0, q1, k1, v1, q2, k2, v2, gate_ref, o_ref, o_scr, l_scr):
    T = o_ref.shape[1]
    n = DIL_BLOCK
    hd = DIL_HEAD_DIM
    qi = lax.broadcasted_iota(jnp.int32, (n, 2 * n), 0)
    ki = lax.broadcasted_iota(jnp.int32, (n, 2 * n), 1)
    band = (ki >= qi) & (ki <= qi + n)
    causal = (lax.broadcasted_iota(jnp.int32, (n, n), 1)
              <= lax.broadcasted_iota(jnp.int32, (n, n), 0))
    groups = ((q0, k0, v0), (q1, k1, v1), (q2, k2, v2))

    dilated = [g for g, (_, r) in enumerate(DIL_GROUPS) if r > 1]
    dense = [g for g, (_, r) in enumerate(DIL_GROUPS) if r == 1]
    assert len(dense) == 1 and o_scr.shape[0] == len(dilated)

    tasks = []
    for g in dilated + dense:
        r = DIL_GROUPS[g][1]
        tasks += [(g, p, j) for j in range(T // r // n) for p in range(r)]
    batches = [tasks[t:t + DIL_BATCH] for t in range(0, len(tasks), DIL_BATCH)]

    def scores(i, _):
        out = []
        for g, p, j in batches[i]:
            qr, kr, vr = groups[g]
            cols = slice(p * hd, (p + 1) * hd)
            win = slice(0, n) if j == 0 else slice((j - 1) * n, (j + 1) * n)
            s = _dot_nt(qr[0, 0, j * n:(j + 1) * n, cols], kr[0, 0, win, cols])
            out.append((jnp.where(causal if j == 0 else band, s, -jnp.inf), vr[0, 0, win, cols]))
        return out

    def row_max(i, carry):
        return [(s, vw, jnp.max(s, axis=-1, keepdims=True)) for s, vw in carry]

    def exponentials(i, carry):
        out = []
        for s, vw, m in carry:
            e = jnp.exp2(s - m)
            out.append((e.astype(BF16), vw, m, jnp.sum(e, axis=-1, keepdims=True)))
        return out

    def weighted_values(i, carry):
        for (g, p, j), (e, vw, m, den) in zip(batches[i], carry):
            o = _dot(e, vw) * (1.0 / den)
            lse = jnp.broadcast_to(m + jnp.log2(den), o.shape)
            r = DIL_GROUPS[g][1]
            if r > 1:
                slot = dilated.index(g)
                dst = pl.ds(j * n * r + p, n, stride=r)
                o_scr[slot, dst, :] = o
                l_scr[slot, dst, :] = lse
            else:
                rows = slice(j * n, (j + 1) * n)
                ls = [lse] + [l_scr[slot, rows, :] for slot in range(len(dilated))]
                os_ = [o] + [o_scr[slot, rows, :] for slot in range(len(dilated))]
                mx = functools.reduce(jnp.maximum, ls)
                ws = [jnp.exp2(l - mx) for l in ls]
                num = functools.reduce(lambda a, b: a + b, [w * ov for w, ov in zip(ws, os_)])
                mixed = num / functools.reduce(lambda a, b: a + b, ws)
                gt = gate_ref[0, rows, :].astype(F32)
                o_ref[0, rows, :] = (mixed * _silu(gt)).astype(BF16)
        return None

    _software_pipeline(len(batches), (scores, row_max, exponentials, weighted_values))


def _dilated(qkv_groups, gate, B, T):
    hd = DIL_HEAD_DIM
    in_specs, args = [], []
    for (_, r), qkv in zip(DIL_GROUPS, qkv_groups):
        view = qkv.reshape(qkv.shape[0], B, T // r, r * hd)
        for s in range(3):
            in_specs.append(pl.BlockSpec((1, 1, T // r, r * hd),
                                         lambda b, h, s=s: (s * DIL_HEADS + h, b, 0, 0)))
            args.append(view)
    in_specs.append(pl.BlockSpec((1, T, hd), lambda b, h: (b, 0, h)))
    args.append(gate.reshape(B, T, gate.shape[-1]))
    n_groups = sum(r > 1 for _, r in DIL_GROUPS)
    return pl.pallas_call(
        _dil_kernel,
        grid=(B, DIL_HEADS),
        in_specs=in_specs,
        out_specs=pl.BlockSpec((1, T, hd), lambda b, h: (b, 0, h)),
        out_shape=jax.ShapeDtypeStruct((B, T, DIL_HEADS * hd), BF16),
        scratch_shapes=[pltpu.VMEM((n_groups, T, hd), F32), pltpu.VMEM((n_groups, T, hd), F32)],
        compiler_params=_params(2),
        name="dilated",
    )(*args)


def _out_kernel(mix_ref, qm_ref, gm_ref, kst_ref, vst_ref, w_ref, x_ref, fnw_ref, o_ref, *, final):
    n_mem = kst_ref.shape[0] // MEM_HEADS
    tm, wm = mix_ref.shape
    split = wm % OUT_SPLIT == 0
    rs = tm if split else OUT_ROWS

    def scores(i, _):
        rows = slice(i * rs, (i + 1) * rs)
        s = _dot_nt(qm_ref[rows, :], kst_ref[...])
        y = _dot(mix_ref[rows, :], w_ref[:wm, :]) if split else None
        return rows, s, y

    def softmax(i, carry):
        rows, s, y = carry
        ps = []
        for hh in range(MEM_HEADS):
            seg = s[:, hh * n_mem:(hh + 1) * n_mem]
            e = jnp.exp(seg - jnp.max(seg, axis=-1, keepdims=True))
            ps.append((e / jnp.sum(e, axis=-1, keepdims=True)).astype(BF16))
        return rows, ps, y

    def values(i, carry):
        rows, ps, y = carry
        mo = None
        for hh, p in enumerate(ps):
            part = _dot(p, vst_ref[hh * n_mem:(hh + 1) * n_mem, :])
            mo = part if mo is None else mo + part
        return rows, (mo * _silu(gm_ref[rows, :].astype(F32))).astype(BF16), y

    def project(i, carry):
        rows, bm, y = carry
        if split:
            y = y + _dot(bm, w_ref[wm:, :]) + x_ref[rows, :]
        else:
            y = _dot(jnp.concatenate([mix_ref[rows, :], bm], axis=-1), w_ref[...]) + x_ref[rows, :]
        if final:
            y = _rmsnorm(y, fnw_ref[...])
        o_ref[rows, :] = y
        return None

    _software_pipeline(tm // rs, (scores, softmax, values, project))


def _out_proj(mix2, qm2, gate2, kv, layer, w_branch, x2, fnw, T, tm, final):
    M, D = x2.shape
    wm = mix2.shape[1]
    gate_blk = (gate2.shape[1] - MEM_WIDTH) // MEM_WIDTH
    tpb = T // tm
    rows_kv = kv.shape[3]
    kv_spec = lambda which: pl.BlockSpec(
        (None, None, None, rows_kv, MEM_WIDTH), lambda i: (layer, which, i // tpb, 0, 0))
    return pl.pallas_call(
        functools.partial(_out_kernel, final=final),
        grid=(M // tm,),
        in_specs=[
            pl.BlockSpec((tm, wm), lambda i: (i, 0)),
            pl.BlockSpec((tm, MEM_WIDTH), lambda i: (i, 0)),
            pl.BlockSpec((tm, MEM_WIDTH), lambda i: (i, gate_blk)),
            kv_spec(0), kv_spec(1),
            pl.BlockSpec((wm + MEM_WIDTH, D), lambda i: (0, 0)),
            pl.BlockSpec((tm, D), lambda i: (i, 0)),
            pl.BlockSpec((1, D), lambda i: (0, 0)),
        ],
        out_specs=pl.BlockSpec((tm, D), lambda i: (i, 0)),
        out_shape=jax.ShapeDtypeStruct((M, D), F32),
        compiler_params=_params(1),
        name="memattn_outproj",
    )(mix2, qm2, gate2, kv, kv, w_branch, x2, fnw.reshape(1, D))


def _pad_heads(w, heads, width, padded, axis=-1):
    axis = axis % w.ndim
    shape = w.shape[:axis] + (heads, width) + w.shape[axis + 1:]
    pads = [(0, 0)] * (w.ndim + 1)
    pads[axis + 1] = (0, padded - width)
    out = jnp.pad(w.reshape(shape), pads)
    return out.reshape(w.shape[:axis] + (heads * padded,) + w.shape[axis + 1:])


def _gla_weights(w_in, w_gate_up, b_gate, gla_norm_w, w_out):
    hk = GLA_HEADS * GLA_DK
    mixw = GLA_HEADS * GLA_DV
    c = [0, hk, 2 * hk, 2 * hk + mixw, 2 * hk + mixw + GLA_RANK, 2 * hk + mixw + GLA_RANK + MEM_WIDTH]
    q, k, v = w_in[:, c[0]:c[1]], w_in[:, c[1]:c[2]], w_in[:, c[2]:c[3]]
    gl, qm, gate = w_in[:, c[3]:c[4]], w_in[:, c[4]:c[5]], w_in[:, c[5]:]
    w = jnp.concatenate([
        _pad_heads(q, GLA_HEADS, GLA_DK, GLA_DK_PAD),
        _pad_heads(k, GLA_HEADS, GLA_DK, GLA_DK_PAD),
        _pad_heads(v, GLA_HEADS, GLA_DV, GLA_DV_PAD),
        jnp.pad(gl, ((0, 0), (0, LANE - GLA_RANK))),
        qm,
        _pad_heads(gate[:, :mixw], GLA_HEADS, GLA_DV, GLA_DV_PAD),
        gate[:, mixw:],
    ], axis=1).astype(BF16)
    widths = (GLA_HEADS * GLA_DK_PAD, GLA_HEADS * GLA_DK_PAD, GLA_HEADS * GLA_DV_PAD, LANE,
              MEM_WIDTH, GLA_HEADS * GLA_DV_PAD + MEM_WIDTH)
    wgu = jnp.pad(_pad_heads(w_gate_up, GLA_HEADS, GLA_DK, GLA_DK_PAD),
                  ((0, LANE - GLA_RANK), (0, 0))).astype(BF16)
    bg = _pad_heads(b_gate.reshape(1, hk), GLA_HEADS, GLA_DK, GLA_DK_PAD)
    nw = jnp.pad(gla_norm_w.reshape(1, GLA_DV), ((0, 0), (0, GLA_DV_PAD - GLA_DV)))
    w_branch = jnp.concatenate(
        [_pad_heads(w_out[:mixw], GLA_HEADS, GLA_DV, GLA_DV_PAD, axis=0), w_out[mixw:]],
        axis=0).astype(BF16)
    return w, widths, wgu, bg, nw, w_branch


def kernel(x, mem, mem_norm_w, norm_w, w_memkv, w_out, w_in_a, w_gate_up, b_gate, gla_norm_w,
           w_in_b, final_norm_w):
    B, T, D = x.shape
    depth = norm_w.shape[0]
    M = B * T
    tm = 512
    kv = _memkv(mem, mem_norm_w, w_memkv)
    x2 = x.reshape(M, D)
    for i in range(depth):
        j = i // 2
        final = i == depth - 1
        if i % 2 == 0:
            w, widths, wgu, bg, nw, w_branch = _gla_weights(
                w_in_a[j], w_gate_up[j], b_gate[j], gla_norm_w[j], w_out[i])
            q, k, v, glow, qm, gate = _inproj_a(x2, norm_w[i], w, widths, tm)
            mix = _gla(q, k, v, glow, gate, wgu, bg, nw, B, T)
        else:
            w_branch = w_out[i].astype(BF16)
            *qkv_groups, qm, gate = _inproj_b(x2, norm_w[i], w_in_b[j].astype(BF16), T, tm)
            mix = _dilated(qkv_groups, gate, B, T)
        x2 = _out_proj(mix.reshape(M, -1), qm, gate, kv, i, w_branch, x2, final_norm_w, T, tm, final)
    return x2.reshape(B, T, D)
```

```python
import functools

import jax
import jax.numpy as jnp
from jax import lax
from jax.experimental import pallas as pl
from jax.experimental.pallas import tpu as pltpu

F32 = jnp.float32
BF16 = jnp.bfloat16

LANE = 128
NORM_EPS = 1e-6
LOG2_E = 1.4426950408889634
MEM_HEADS = 4
MEM_HEAD_DIM = 64
MEM_WIDTH = MEM_HEADS * MEM_HEAD_DIM
GLA_HEADS = 4
GLA_DK = 96
GLA_DV = 192
GLA_DK_PAD = 128
GLA_DV_PAD = 256
GLA_RANK = 16
GLA_TAU = 16.0
GLA_CHUNK = 64
GLA_UNROLL = 4
DIL_GROUPS = ((128, 1), (512, 4), (2048, 16))
DIL_HEADS = 6
DIL_HEAD_DIM = 128
DIL_BLOCK = 128
DIL_BATCH = 2
OUT_ROWS = 256
OUT_TILE = 1024
ROPE_THETA = 500000.0
ROPE_DIM = 32
VMEM_LIMIT = 56 * 1024 * 1024

_NT = (((1,), (1,)), ((), ()))
_TN = (((0,), (0,)), ((), ()))


def _dot(a, b):
    return jnp.dot(a, b, preferred_element_type=F32)


def _dot_nt(a, b):
    return lax.dot_general(a, b, _NT, preferred_element_type=F32)


def _dot_tn(a, b):
    return lax.dot_general(a, b, _TN, preferred_element_type=F32)


def _rmsnorm(x, w):
    ms = jnp.mean(x * x, axis=-1, keepdims=True)
    return x * lax.rsqrt(ms + NORM_EPS) * w


def _silu(g):
    return g * (0.5 + 0.5 * jnp.tanh(0.5 * g))


def _software_pipeline(n_items, stages):
    carries = [None] * n_items
    for t in range(n_items + len(stages) - 1):
        for k, stage in enumerate(stages):
            i = t - k
            if 0 <= i < n_items:
                carries[i] = stage(i, carries[i])


def _params(n_parallel):
    return pltpu.CompilerParams(
        dimension_semantics=("parallel",) * n_parallel,
        vmem_limit_bytes=VMEM_LIMIT)


def _memkv_kernel(mem_ref, nw_ref, w_ref, kv_ref, *, depth):
    h = _rmsnorm(mem_ref[0], nw_ref[...]).astype(BF16)
    kv = _dot(h, w_ref[...])
    n_mem = kv.shape[0]
    lane = lax.broadcasted_iota(jnp.int32, (n_mem, MEM_WIDTH), 1)
    for l in range(depth):
        base = l * 2 * MEM_WIDTH
        k = kv[:, base:base + MEM_WIDTH] * (MEM_HEAD_DIM ** -0.5)
        v = kv[:, base + MEM_WIDTH:base + 2 * MEM_WIDTH]
        for hh in range(MEM_HEADS):
            msk = (lane >= hh * MEM_HEAD_DIM) & (lane < (hh + 1) * MEM_HEAD_DIM)
            rows = slice(hh * n_mem, (hh + 1) * n_mem)
            kv_ref[l, 0, 0, rows, :] = jnp.where(msk, k, 0.0).astype(BF16)
            kv_ref[l, 1, 0, rows, :] = jnp.where(msk, v, 0.0).astype(BF16)


def _memkv(mem, mem_norm_w, w_memkv):
    B, n_mem, D = mem.shape
    depth = w_memkv.shape[0]
    w = jnp.transpose(w_memkv, (1, 0, 2)).reshape(D, depth * 2 * MEM_WIDTH).astype(BF16)
    return pl.pallas_call(
        functools.partial(_memkv_kernel, depth=depth),
        grid=(B,),
        in_specs=[
            pl.BlockSpec((1, n_mem, D), lambda b: (b, 0, 0)),
            pl.BlockSpec((1, D), lambda b: (0, 0)),
            pl.BlockSpec((D, depth * 2 * MEM_WIDTH), lambda b: (0, 0)),
        ],
        out_specs=pl.BlockSpec((depth, 2, 1, MEM_HEADS * n_mem, MEM_WIDTH),
                               lambda b: (0, 0, b, 0, 0)),
        out_shape=jax.ShapeDtypeStruct((depth, 2, B, MEM_HEADS * n_mem, MEM_WIDTH), BF16),
        compiler_params=_params(1),
        name="memkv",
    )(mem, mem_norm_w.reshape(1, D), w)


def _inproj_a_kernel(x_ref, nw_ref, w_ref, q_ref, k_ref, v_ref, gl_ref, qm_ref, gate_ref):
    h = _rmsnorm(x_ref[...], nw_ref[...]).astype(BF16)
    tm = h.shape[0]
    hk = GLA_HEADS * GLA_DK
    mixw = GLA_HEADS * GLA_DV

    def pad_heads(val, width, padded):
        zeros = jnp.zeros((tm, padded - width), val.dtype)
        pieces = []
        for hh in range(GLA_HEADS):
            pieces += [val[:, hh * width:(hh + 1) * width], zeros]
        return jnp.concatenate(pieces, axis=1).astype(BF16)

    c0 = 0
    qk = _dot(h, w_ref[:, c0:c0 + 2 * hk])
    q_ref[...] = pad_heads(qk[:, :hk], GLA_DK, GLA_DK_PAD)
    k_ref[...] = pad_heads(qk[:, hk:], GLA_DK, GLA_DK_PAD)
    c0 += 2 * hk
    v_ref[...] = pad_heads(_dot(h, w_ref[:, c0:c0 + mixw]), GLA_DV, GLA_DV_PAD)
    c0 += mixw
    gate = _dot(h, w_ref[:, c0:c0 + mixw + MEM_WIDTH])
    gate_ref[:, :GLA_HEADS * GLA_DV_PAD] = pad_heads(gate[:, :mixw], GLA_DV, GLA_DV_PAD)
    gate_ref[:, GLA_HEADS * GLA_DV_PAD:] = gate[:, mixw:].astype(BF16)
    c0 += mixw + MEM_WIDTH
    qm_ref[...] = _dot(h, w_ref[:, c0:c0 + MEM_WIDTH]).astype(BF16)
    c0 += MEM_WIDTH
    gl_ref[...] = _dot(h, w_ref[:, c0:]).astype(BF16)


def _inproj_a(x2, norm_w, w, widths, tm):
    M, D = x2.shape
    N = w.shape[1]
    return pl.pallas_call(
        _inproj_a_kernel,
        grid=(M // tm,),
        in_specs=[
            pl.BlockSpec((tm, D), lambda i: (i, 0)),
            pl.BlockSpec((1, D), lambda i: (0, 0)),
            pl.BlockSpec((D, N), lambda i: (0, 0)),
        ],
        out_specs=[pl.BlockSpec((tm, cw), lambda i: (i, 0)) for cw in widths],
        out_shape=[jax.ShapeDtypeStruct((M, cw), BF16) for cw in widths],
        compiler_params=_params(1),
        name="inproj_gla",
    )(x2, norm_w.reshape(1, D), w)


def _inproj_b_kernel(x_ref, nw_ref, w_ref, rope_ref, qkv0_ref, qkv1_ref, qkv2_ref, qm_ref, gate_ref,
                     h_scr):
    hf = _rmsnorm(x_ref[...], nw_ref[...])
    h = hf.astype(BF16)
    tm, D = hf.shape
    hd = DIL_HEAD_DIM
    hw = DIL_HEADS * hd
    n_lt = D // LANE
    for lt in range(n_lt):
        h_scr[lt] = hf[:, lt * LANE:(lt + 1) * LANE]
    group_refs = (qkv0_ref, qkv1_ref, qkv2_ref)
    for g, (_, r) in enumerate(DIL_GROUPS):
        out_ref = group_refs[g]
        rows_p = tm // r
        if r == 1:
            hg = h
        else:
            hg = jnp.concatenate(
                [jnp.concatenate([h_scr[lt, pl.ds(p, rows_p, stride=r), :] for lt in range(n_lt)],
                                 axis=1).astype(BF16) for p in range(r)], axis=0)
        cos, sin_up, sin_dn = rope_ref[3 * g], rope_ref[3 * g + 1], rope_ref[3 * g + 2]
        for s in range(3):
            ci = g * 3 + s
            acc = _dot(hg, w_ref[:, ci * hw:(ci + 1) * hw])
            for hh in range(DIL_HEADS):
                slab = acc[:, hh * hd:(hh + 1) * hd]
                if s < 2:
                    slab = (slab * cos + pltpu.roll(slab, ROPE_DIM // 2, 1) * sin_up
                            + pltpu.roll(slab, LANE - ROPE_DIM // 2, 1) * sin_dn)
                if s == 0:
                    slab = slab * (hd ** -0.5 * LOG2_E)
                slab = slab.astype(BF16)
                if r == 1:
                    out_ref[s * DIL_HEADS + hh] = slab
                else:
                    for p in range(r):
                        out_ref[s * DIL_HEADS + hh, :, p * hd:(p + 1) * hd] = (
                            slab[p * rows_p:(p + 1) * rows_p, :])
    c0 = len(DIL_GROUPS) * 3 * hw
    qm_ref[...] = _dot(h, w_ref[:, c0:c0 + MEM_WIDTH]).astype(BF16)
    c0 += MEM_WIDTH
    gate_ref[...] = _dot(h, w_ref[:, c0:]).astype(BF16)


def _rope_tables(T, tm):
    half = ROPE_DIM // 2
    inv = ROPE_THETA ** (-jnp.arange(half, dtype=F32) / half)
    ang = jnp.arange(T).astype(F32)[:, None] * inv[None, :]
    cos, sin = jnp.cos(ang), jnp.sin(ang)
    ones = jnp.ones((T, LANE - ROPE_DIM), F32)
    zeros = jnp.zeros((T, LANE - ROPE_DIM), F32)
    zh = jnp.zeros((T, half), F32)
    base = [jnp.concatenate([cos, cos, ones], axis=-1),
            jnp.concatenate([zh, sin, zeros], axis=-1),
            jnp.concatenate([-sin, zh, zeros], axis=-1)]
    tabs = []
    for _, r in DIL_GROUPS:
        for t in base:
            tabs.append(t.reshape(T // tm, tm // r, r, LANE).transpose(0, 2, 1, 3).reshape(T, LANE))
    return jnp.stack(tabs)


def _inproj_b(x2, norm_w, w, T, tm):
    M, D = x2.shape
    N = w.shape[1]
    hd = DIL_HEAD_DIM
    heads_per_group = 3 * DIL_HEADS
    n_tabs = 3 * len(DIL_GROUPS)
    gate_w = N - len(DIL_GROUPS) * heads_per_group * hd - MEM_WIDTH
    tpb = T // tm
    qkv_specs = [pl.BlockSpec((heads_per_group, tm // r, r * hd), lambda i: (0, i, 0))
                 for _, r in DIL_GROUPS]
    qkv_shapes = [jax.ShapeDtypeStruct((heads_per_group, M // r, r * hd), BF16)
                  for _, r in DIL_GROUPS]
    return pl.pallas_call(
        _inproj_b_kernel,
        grid=(M // tm,),
        in_specs=[
            pl.BlockSpec((tm, D), lambda i: (i, 0)),
            pl.BlockSpec((1, D), lambda i: (0, 0)),
            pl.BlockSpec((D, N), lambda i: (0, 0), pipeline_mode=pl.Buffered(1)),
            pl.BlockSpec((n_tabs, tm, LANE), lambda i: (0, i % tpb, 0)),
        ],
        out_specs=qkv_specs + [
            pl.BlockSpec((tm, MEM_WIDTH), lambda i: (i, 0)),
            pl.BlockSpec((tm, gate_w), lambda i: (i, 0)),
        ],
        out_shape=qkv_shapes + [
            jax.ShapeDtypeStruct((M, MEM_WIDTH), BF16),
            jax.ShapeDtypeStruct((M, gate_w), BF16),
        ],
        scratch_shapes=[pltpu.VMEM((D // LANE, tm, LANE), F32)],
        compiler_params=_params(1),
        name="inproj_dil",
    )(x2, norm_w.reshape(1, D), w, _rope_tables(T, tm))


def _gla_kernel(q_ref, k_ref, v_ref, gl_ref, gate_ref, wgu_ref, bg_ref, nw_ref, o_ref):
    T = q_ref.shape[1]
    C = GLA_CHUNK
    U = GLA_UNROLL
    R = U * C
    row = lax.broadcasted_iota(jnp.int32, (R, R), 0)
    col = lax.broadcasted_iota(jnp.int32, (R, R), 1)
    shift = C.bit_length() - 1
    in_chunk_causal = ((row >> shift) == (col >> shift)) & (row >= col)
    tri = jnp.where(in_chunk_causal, 1.0, 0.0).astype(BF16)
    state = [jnp.zeros((v_ref.shape[-1], q_ref.shape[-1]), F32)]

    def gate_logits(i, _):
        rows = slice(i * R, (i + 1) * R)
        z = _dot(gl_ref[0, rows, :], wgu_ref[...]) + bg_ref[...]
        la = (jnp.minimum(z, 0.0) * (LOG2_E / GLA_TAU)
              - jnp.log2(1.0 + jnp.exp2(jnp.abs(z) * -LOG2_E)) * (1.0 / GLA_TAU))
        hi = la.astype(BF16)
        rem = la - hi.astype(F32)
        mid = rem.astype(BF16)
        lo = (rem - mid.astype(F32)).astype(BF16)
        return rows, hi, mid, lo

    def decays(i, carry):
        rows, hi, mid, lo = carry
        b = _dot(tri, hi) + _dot(tri, mid) + _dot(tri, lo)
        b_last = [b[c * C + C - 1:(c + 1) * C, :] for c in range(U)]
        bl_rows = jnp.concatenate([jnp.broadcast_to(bl, (C, bl.shape[-1])) for bl in b_last], axis=0)
        qc = q_ref[0, rows, :].astype(F32) * (GLA_DK ** -0.5)
        kc = k_ref[0, rows, :].astype(F32)
        q_in = (qc * jnp.exp2(b)).astype(BF16)
        k_in = (kc * jnp.exp2(-b)).astype(BF16)
        k_out = (kc * jnp.exp2(bl_rows - b)).astype(BF16)
        return rows, q_in, k_in, k_out, [jnp.exp2(bl) for bl in b_last]

    def intra(i, carry):
        rows, q_in, k_in, k_out, dec = carry
        vc = v_ref[0, rows, :]
        a = _dot_nt(q_in, k_in)
        ds = [_dot_tn(vc[c * C:(c + 1) * C], k_out[c * C:(c + 1) * C]) for c in range(U)]
        o_intra = _dot(jnp.where(in_chunk_causal, a, 0.0).astype(BF16), vc)
        return rows, q_in, o_intra, ds, dec

    def inter(i, carry):
        rows, q_in, o_intra, ds, dec = carry
        st = state[0]
        o_inter = []
        for c in range(U):
            o_inter.append(_dot_nt(q_in[c * C:(c + 1) * C], st.astype(BF16)))
            st = st * dec[c] + ds[c]
        state[0] = st
        o = o_intra + jnp.concatenate(o_inter, axis=0)
        ms = jnp.sum(o * o, axis=-1, keepdims=True) * (1.0 / GLA_DV)
        y = o * lax.rsqrt(ms + NORM_EPS) * nw_ref[...]
        g = gate_ref[0, rows, :].astype(F32)
        o_ref[0, rows, :] = (y * _silu(g)).astype(BF16)
        return None

    _software_pipeline(T // R, (gate_logits, decays, intra, inter))


def _gla(q, k, v, glow, gate, wgu, bg, nw, B, T):
    kw, vw = GLA_DK_PAD, GLA_DV_PAD
    q3 = q.reshape(B, T, GLA_HEADS * kw)
    k3 = k.reshape(B, T, GLA_HEADS * kw)
    v3 = v.reshape(B, T, GLA_HEADS * vw)
    g3 = glow.reshape(B, T, LANE)
    gate3 = gate.reshape(B, T, gate.shape[-1])
    return pl.pallas_call(
        _gla_kernel,
        grid=(B, GLA_HEADS),
        in_specs=[
            pl.BlockSpec((1, T, kw), lambda b, h: (b, 0, h)),
            pl.BlockSpec((1, T, kw), lambda b, h: (b, 0, h)),
            pl.BlockSpec((1, T, vw), lambda b, h: (b, 0, h)),
            pl.BlockSpec((1, T, LANE), lambda b, h: (b, 0, 0)),
            pl.BlockSpec((1, T, vw), lambda b, h: (b, 0, h)),
            pl.BlockSpec((LANE, kw), lambda b, h: (0, h)),
            pl.BlockSpec((1, kw), lambda b, h: (0, h)),
            pl.BlockSpec((1, vw), lambda b, h: (0, 0)),
        ],
        out_specs=pl.BlockSpec((1, T, vw), lambda b, h: (b, 0, h)),
        out_shape=jax.ShapeDtypeStruct((B, T, GLA_HEADS * vw), BF16),
        compiler_params=_params(2),
        name="gla",
    )(q3, k3, v3, g3, gate3, wgu, bg, nw)


def _dil_kernel(q0, k0, v0, q1, k1, v1, q2, k2, v2, gate_ref, o_ref, o_scr, l_scr):
    T = o_ref.shape[1]
    n = DIL_BLOCK
    hd = DIL_HEAD_DIM
    qi = lax.broadcasted_iota(jnp.int32, (n, 2 * n), 0)
    ki = lax.broadcasted_iota(jnp.int32, (n, 2 * n), 1)
    band = (ki >= qi) & (ki <= qi + n)
    causal = (lax.broadcasted_iota(jnp.int32, (n, n), 1)
              <= lax.broadcasted_iota(jnp.int32, (n, n), 0))
    groups = ((q0, k0, v0), (q1, k1, v1), (q2, k2, v2))

    dilated = [g for g, (_, r) in enumerate(DIL_GROUPS) if r > 1]
    dense = [g for g, (_, r) in enumerate(DIL_GROUPS) if r == 1]
    assert len(dense) == 1 and o_scr.shape[0] == len(dilated)

    tasks = []
    for g in dilated + dense:
        r = DIL_GROUPS[g][1]
        tasks += [(g, p, j) for j in range(T // r // n) for p in range(r)]
    batches = [tasks[t:t + DIL_BATCH] for t in range(0, len(tasks), DIL_BATCH)]

    def scores(i, _):
        out = []
        for g, p, j in batches[i]:
            qr, kr, vr = groups[g]
            cols = slice(p * hd, (p + 1) * hd)
            win = slice(0, n) if j == 0 else slice((j - 1) * n, (j + 1) * n)
            s = _dot_nt(qr[0, 0, j * n:(j + 1) * n, cols], kr[0, 0, win, cols])
            out.append((jnp.where(causal if j == 0 else band, s, -jnp.inf), vr[0, 0, win, cols]))
        return out

    def row_max(i, carry):
        return [(s, vw, jnp.max(s, axis=-1, keepdims=True)) for s, vw in carry]

    def exponentials(i, carry):
        out = []
        for s, vw, m in carry:
            e = jnp.exp2(s - m)
            out.append((e.astype(BF16), vw, m, jnp.sum(e, axis=-1, keepdims=True)))
        return out

    def weighted_values(i, carry):
        for (g, p, j), (e, vw, m, den) in zip(batches[i], carry):
            o = _dot(e, vw) * (1.0 / den)
            lse = jnp.broadcast_to(m + jnp.log2(den), o.shape)
            r = DIL_GROUPS[g][1]
            if r > 1:
                slot = dilated.index(g)
                dst = pl.ds(j * n * r + p, n, stride=r)
                o_scr[slot, dst, :] = o
                l_scr[slot, dst, :] = lse
            else:
                rows = slice(j * n, (j + 1) * n)
                ls = [lse] + [l_scr[slot, rows, :] for slot in range(len(dilated))]
                os_ = [o] + [o_scr[slot, rows, :] for slot in range(len(dilated))]
                mx = functools.reduce(jnp.maximum, ls)
                ws = [jnp.exp2(l - mx) for l in ls]
                num = functools.reduce(lambda a, b: a + b, [w * ov for w, ov in zip(ws, os_)])
                mixed = num / functools.reduce(lambda a, b: a + b, ws)
                gt = gate_ref[0, rows, :].astype(F32)
                o_ref[0, rows, :] = (mixed * _silu(gt)).astype(BF16)
        return None

    _software_pipeline(len(batches), (scores, row_max, exponentials, weighted_values))


def _dilated(qkv_groups, gate, B, T):
    hd = DIL_HEAD_DIM
    in_specs, args = [], []
    for (_, r), qkv in zip(DIL_GROUPS, qkv_groups):
        view = qkv.reshape(qkv.shape[0], B, T // r, r * hd)
        for s in range(3):
            in_specs.append(pl.BlockSpec((1, 1, T // r, r * hd),
                                         lambda b, h, s=s: (s * DIL_HEADS + h, b, 0, 0)))
            args.append(view)
    in_specs.append(pl.BlockSpec((1, T, hd), lambda b, h: (b, 0, h)))
    args.append(gate.reshape(B, T, gate.shape[-1]))
    n_groups = sum(r > 1 for _, r in DIL_GROUPS)
    return pl.pallas_call(
        _dil_kernel,
        grid=(B, DIL_HEADS),
        in_specs=in_specs,
        out_specs=pl.BlockSpec((1, T, hd), lambda b, h: (b, 0, h)),
        out_shape=jax.ShapeDtypeStruct((B, T, DIL_HEADS * hd), BF16),
        scratch_shapes=[pltpu.VMEM((n_groups, T, hd), F32), pltpu.VMEM((n_groups, T, hd), F32)],
        compiler_params=_params(2),
        name="dilated",
    )(*args)


def _out_kernel(mix_ref, qm_ref, gm_ref, kst_ref, vst_ref, w_ref, x_ref, fnw_ref, o_ref, *,
                final, mix_pad):
    n_mem = kst_ref.shape[0] // MEM_HEADS
    tm = mix_ref.shape[0]
    rs = OUT_ROWS

    def mixer_rows(rows):
        mix = mix_ref[rows, :]
        if mix_pad is None:
            return mix
        heads, width, padded = mix_pad
        return jnp.concatenate([mix[:, hh * padded:hh * padded + width] for hh in range(heads)], axis=1)

    def scores(i, _):
        rows = slice(i * rs, (i + 1) * rs)
        return rows, _dot_nt(qm_ref[rows, :], kst_ref[...])

    def softmax(i, carry):
        rows, s = carry
        ps = []
        for hh in range(MEM_HEADS):
            seg = s[:, hh * n_mem:(hh + 1) * n_mem]
            e = jnp.exp(seg - jnp.max(seg, axis=-1, keepdims=True))
            ps.append((e / jnp.sum(e, axis=-1, keepdims=True)).astype(BF16))
        return rows, ps

    def values(i, carry):
        rows, ps = carry
        mo = None
        for hh, p in enumerate(ps):
            part = _dot(p, vst_ref[hh * n_mem:(hh + 1) * n_mem, :])
            mo = part if mo is None else mo + part
        return rows, (mo * _silu(gm_ref[rows, :].astype(F32))).astype(BF16)

    def project(i, carry):
        rows, bm = carry
        branch = jnp.concatenate([mixer_rows(rows), bm], axis=-1)
        y = _dot(branch, w_ref[...]) + x_ref[rows, :]
        if final:
            y = _rmsnorm(y, fnw_ref[...])
        o_ref[rows, :] = y
        return None

    _software_pipeline(tm // rs, (scores, softmax, values, project))


def _out_proj(mix2, qm2, gate2, kv, layer, w_out, x2, fnw, T, tm, final, mix_pad=None):
    M, D = x2.shape
    wm = mix2.shape[1]
    gate_blk = (gate2.shape[1] - MEM_WIDTH) // MEM_WIDTH
    tpb = T // tm
    rows_kv = kv.shape[3]
    kv_spec = lambda which: pl.BlockSpec(
        (None, None, None, rows_kv, MEM_WIDTH), lambda i: (layer, which, i // tpb, 0, 0))
    return pl.pallas_call(
        functools.partial(_out_kernel, final=final, mix_pad=mix_pad),
        grid=(M // tm,),
        in_specs=[
            pl.BlockSpec((tm, wm), lambda i: (i, 0)),
            pl.BlockSpec((tm, MEM_WIDTH), lambda i: (i, 0)),
            pl.BlockSpec((tm, MEM_WIDTH), lambda i: (i, gate_blk)),
            kv_spec(0), kv_spec(1),
            pl.BlockSpec(w_out.shape, lambda i: (0, 0)),
            pl.BlockSpec((tm, D), lambda i: (i, 0)),
            pl.BlockSpec((1, D), lambda i: (0, 0)),
        ],
        out_specs=pl.BlockSpec((tm, D), lambda i: (i, 0)),
        out_shape=jax.ShapeDtypeStruct((M, D), F32),
        compiler_params=_params(1),
        name="memattn_outproj",
    )(mix2, qm2, gate2, kv, kv, w_out, x2, fnw.reshape(1, D))


def _pad_heads(w, heads, width, padded, axis=-1):
    axis = axis % w.ndim
    shape = w.shape[:axis] + (heads, width) + w.shape[axis + 1:]
    pads = [(0, 0)] * (w.ndim + 1)
    pads[axis + 1] = (0, padded - width)
    out = jnp.pad(w.reshape(shape), pads)
    return out.reshape(w.shape[:axis] + (heads * padded,) + w.shape[axis + 1:])


def _gla_weights(w_in, w_gate_up, b_gate, gla_norm_w):
    hk = GLA_HEADS * GLA_DK
    mixw = GLA_HEADS * GLA_DV
    c = [0, hk, 2 * hk, 2 * hk + mixw, 2 * hk + mixw + GLA_RANK, 2 * hk + mixw + GLA_RANK + MEM_WIDTH]
    q, k, v = w_in[:, c[0]:c[1]], w_in[:, c[1]:c[2]], w_in[:, c[2]:c[3]]
    gl, qm, gate = w_in[:, c[3]:c[4]], w_in[:, c[4]:c[5]], w_in[:, c[5]:]
    w = jnp.concatenate([q, k, v, gate, qm, jnp.pad(gl, ((0, 0), (0, LANE - GLA_RANK)))],
                        axis=1).astype(BF16)
    widths = (GLA_HEADS * GLA_DK_PAD, GLA_HEADS * GLA_DK_PAD, GLA_HEADS * GLA_DV_PAD, LANE,
              MEM_WIDTH, GLA_HEADS * GLA_DV_PAD + MEM_WIDTH)
    wgu = jnp.pad(_pad_heads(w_gate_up, GLA_HEADS, GLA_DK, GLA_DK_PAD),
                  ((0, LANE - GLA_RANK), (0, 0))).astype(BF16)
    bg = _pad_heads(b_gate.reshape(1, hk), GLA_HEADS, GLA_DK, GLA_DK_PAD)
    nw = jnp.pad(gla_norm_w.reshape(1, GLA_DV), ((0, 0), (0, GLA_DV_PAD - GLA_DV)))
    return w, widths, wgu, bg, nw


def kernel(x, mem, mem_norm_w, norm_w, w_memkv, w_out, w_in_a, w_gate_up, b_gate, gla_norm_w,
           w_in_b, final_norm_w):
    B, T, D = x.shape
    depth = norm_w.shape[0]
    M = B * T
    tm = 512
    kv = _memkv(mem, mem_norm_w, w_memkv)
    x2 = x.reshape(M, D)
    for i in range(depth):
        j = i // 2
        final = i == depth - 1
        if i % 2 == 0:
            w, widths, wgu, bg, nw = _gla_weights(w_in_a[j], w_gate_up[j], b_gate[j], gla_norm_w[j])
            mix_pad = (GLA_HEADS, GLA_DV, GLA_DV_PAD)
            q, k, v, glow, qm, gate = _inproj_a(x2, norm_w[i], w, widths, tm)
            mix = _gla(q, k, v, glow, gate, wgu, bg, nw, B, T)
        else:
            mix_pad = None
            *qkv_groups, qm, gate = _inproj_b(x2, norm_w[i], w_in_b[j].astype(BF16), T, tm)
            mix = _dilated(qkv_groups, gate, B, T)
        x2 = _out_proj(mix.reshape(M, -1), qm, gate, kv, i, w_out[i].astype(BF16), x2, final_norm_w,
                       T, OUT_TILE, final, mix_pad)
    return x2.reshape(B, T, D)
```

```python
import functools

import jax
import jax.numpy as jnp
from jax import lax
from jax.experimental import pallas as pl
from jax.experimental.pallas import tpu as pltpu

F32 = jnp.float32
BF16 = jnp.bfloat16

LANE = 128
NORM_EPS = 1e-6
LOG2_E = 1.4426950408889634
MEM_HEADS = 4
MEM_HEAD_DIM = 64
MEM_WIDTH = MEM_HEADS * MEM_HEAD_DIM
GLA_HEADS = 4
GLA_DK = 96
GLA_DV = 192
GLA_DK_PAD = 128
GLA_DV_PAD = 256
GLA_RANK = 16
GLA_TAU = 16.0
GLA_CHUNK = 64
GLA_UNROLL = 2
DIL_GROUPS = ((128, 1), (512, 4), (2048, 16))
DIL_HEADS = 6
DIL_HEAD_DIM = 128
DIL_BLOCK = 128
DIL_BATCH = 2
OUT_ROWS = 256
OUT_TILE = 1024
ROPE_THETA = 500000.0
ROPE_DIM = 32
VMEM_LIMIT = 56 * 1024 * 1024

_NT = (((1,), (1,)), ((), ()))
_TN = (((0,), (0,)), ((), ()))


def _dot(a, b):
    return jnp.dot(a, b, preferred_element_type=F32)


def _dot_nt(a, b):
    return lax.dot_general(a, b, _NT, preferred_element_type=F32)


def _dot_tn(a, b):
    return lax.dot_general(a, b, _TN, preferred_element_type=F32)


def _rmsnorm(x, w):
    ms = jnp.mean(x * x, axis=-1, keepdims=True)
    return x * lax.rsqrt(ms + NORM_EPS) * w


def _silu(g):
    return g * (0.5 + 0.5 * jnp.tanh(0.5 * g))


def _software_pipeline(n_items, stages):
    carries = [None] * n_items
    for t in range(n_items + len(stages) - 1):
        for k, stage in enumerate(stages):
            i = t - k
            if 0 <= i < n_items:
                carries[i] = stage(i, carries[i])


def _params(n_parallel):
    return pltpu.CompilerParams(
        dimension_semantics=("parallel",) * n_parallel,
        vmem_limit_bytes=VMEM_LIMIT)


def _memkv_kernel(mem_ref, nw_ref, w_ref, kv_ref, *, depth):
    h = _rmsnorm(mem_ref[0], nw_ref[...]).astype(BF16)
    kv = _dot(h, w_ref[...])
    for l in range(depth):
        base = l * 2 * MEM_WIDTH
        k = kv[:, base:base + MEM_WIDTH] * (MEM_HEAD_DIM ** -0.5)
        kv_ref[l, 0, 0] = k.astype(BF16)
        kv_ref[l, 1, 0] = kv[:, base + MEM_WIDTH:base + 2 * MEM_WIDTH].astype(BF16)


def _memkv(mem, mem_norm_w, w_memkv):
    B, n_mem, D = mem.shape
    depth = w_memkv.shape[0]
    w = jnp.transpose(w_memkv, (1, 0, 2)).reshape(D, depth * 2 * MEM_WIDTH).astype(BF16)
    return pl.pallas_call(
        functools.partial(_memkv_kernel, depth=depth),
        grid=(B,),
        in_specs=[
            pl.BlockSpec((1, n_mem, D), lambda b: (b, 0, 0)),
            pl.BlockSpec((1, D), lambda b: (0, 0)),
            pl.BlockSpec((D, depth * 2 * MEM_WIDTH), lambda b: (0, 0)),
        ],
        out_specs=pl.BlockSpec((depth, 2, 1, n_mem, MEM_WIDTH), lambda b: (0, 0, b, 0, 0)),
        out_shape=jax.ShapeDtypeStruct((depth, 2, B, n_mem, MEM_WIDTH), BF16),
        compiler_params=_params(1),
        name="memkv",
    )(mem, mem_norm_w.reshape(1, D), w)


def _inproj_a_kernel(x_ref, nw_ref, w_ref, q_ref, k_ref, v_ref, gl_ref, qm_ref, gate_ref):
    h = _rmsnorm(x_ref[...], nw_ref[...]).astype(BF16)
    tm = h.shape[0]
    hk = GLA_HEADS * GLA_DK
    mixw = GLA_HEADS * GLA_DV

    def pad_heads(val, width, padded):
        zeros = jnp.zeros((tm, padded - width), val.dtype)
        pieces = []
        for hh in range(GLA_HEADS):
            pieces += [val[:, hh * width:(hh + 1) * width], zeros]
        return jnp.concatenate(pieces, axis=1).astype(BF16)

    c0 = 0
    qk = _dot(h, w_ref[:, c0:c0 + 2 * hk])
    q_ref[...] = pad_heads(qk[:, :hk], GLA_DK, GLA_DK_PAD)
    k_ref[...] = pad_heads(qk[:, hk:], GLA_DK, GLA_DK_PAD)
    c0 += 2 * hk
    v_ref[...] = pad_heads(_dot(h, w_ref[:, c0:c0 + mixw]), GLA_DV, GLA_DV_PAD)
    c0 += mixw
    gate = _dot(h, w_ref[:, c0:c0 + mixw + MEM_WIDTH])
    gate_ref[:, :GLA_HEADS * GLA_DV_PAD] = pad_heads(gate[:, :mixw], GLA_DV, GLA_DV_PAD)
    gate_ref[:, GLA_HEADS * GLA_DV_PAD:] = gate[:, mixw:].astype(BF16)
    c0 += mixw + MEM_WIDTH
    qm_ref[...] = _dot(h, w_ref[:, c0:c0 + MEM_WIDTH]).astype(BF16)
    c0 += MEM_WIDTH
    gl_ref[...] = _dot(h, w_ref[:, c0:]).astype(BF16)


def _inproj_a(x2, norm_w, w, widths, tm):
    M, D = x2.shape
    N = w.shape[1]
    return pl.pallas_call(
        _inproj_a_kernel,
        grid=(M // tm,),
        in_specs=[
            pl.BlockSpec((tm, D), lambda i: (i, 0)),
            pl.BlockSpec((1, D), lambda i: (0, 0)),
            pl.BlockSpec((D, N), lambda i: (0, 0)),
        ],
        out_specs=[pl.BlockSpec((tm, cw), lambda i: (i, 0)) for cw in widths],
        out_shape=[jax.ShapeDtypeStruct((M, cw), BF16) for cw in widths],
        compiler_params=_params(1),
        name="inproj_gla",
    )(x2, norm_w.reshape(1, D), w)


def _inproj_b_kernel(x_ref, nw_ref, w_ref, rope_ref, qkv0_ref, qkv1_ref, qkv2_ref, qm_ref, gate_ref,
                     h_scr):
    hf = _rmsnorm(x_ref[...], nw_ref[...])
    h = hf.astype(BF16)
    tm, D = hf.shape
    hd = DIL_HEAD_DIM
    hw = DIL_HEADS * hd
    n_lt = D // LANE
    for lt in range(n_lt):
        h_scr[lt] = hf[:, lt * LANE:(lt + 1) * LANE]
    group_refs = (qkv0_ref, qkv1_ref, qkv2_ref)
    for g, (_, r) in enumerate(DIL_GROUPS):
        out_ref = group_refs[g]
        rows_p = tm // r
        if r == 1:
            hg = h
        else:
            hg = jnp.concatenate(
                [jnp.concatenate([h_scr[lt, pl.ds(p, rows_p, stride=r), :] for lt in range(n_lt)],
                                 axis=1).astype(BF16) for p in range(r)], axis=0)
        cos, sin_up, sin_dn = rope_ref[3 * g], rope_ref[3 * g + 1], rope_ref[3 * g + 2]
        for s in range(3):
            ci = g * 3 + s
            acc = _dot(hg, w_ref[:, ci * hw:(ci + 1) * hw])
            for hh in range(DIL_HEADS):
                slab = acc[:, hh * hd:(hh + 1) * hd]
                if s < 2:
                    slab = (slab * cos + pltpu.roll(slab, ROPE_DIM // 2, 1) * sin_up
                            + pltpu.roll(slab, LANE - ROPE_DIM // 2, 1) * sin_dn)
                if s == 0:
                    slab = slab * (hd ** -0.5 * LOG2_E)
                slab = slab.astype(BF16)
                if r == 1:
                    out_ref[s * DIL_HEADS + hh] = slab
                else:
                    for p in range(r):
                        out_ref[s * DIL_HEADS + hh, :, p * hd:(p + 1) * hd] = (
                            slab[p * rows_p:(p + 1) * rows_p, :])
    c0 = len(DIL_GROUPS) * 3 * hw
    qm_ref[...] = _dot(h, w_ref[:, c0:c0 + MEM_WIDTH]).astype(BF16)
    c0 += MEM_WIDTH
    gate_ref[...] = _dot(h, w_ref[:, c0:]).astype(BF16)


def _rope_tables(T, tm):
    half = ROPE_DIM // 2
    inv = ROPE_THETA ** (-jnp.arange(half, dtype=F32) / half)
    ang = jnp.arange(T).astype(F32)[:, None] * inv[None, :]
    cos, sin = jnp.cos(ang), jnp.sin(ang)
    ones = jnp.ones((T, LANE - ROPE_DIM), F32)
    zeros = jnp.zeros((T, LANE - ROPE_DIM), F32)
    zh = jnp.zeros((T, half), F32)
    base = [jnp.concatenate([cos, cos, ones], axis=-1),
            jnp.concatenate([zh, sin, zeros], axis=-1),
            jnp.concatenate([-sin, zh, zeros], axis=-1)]
    tabs = []
    for _, r in DIL_GROUPS:
        for t in base:
            tabs.append(t.reshape(T // tm, tm // r, r, LANE).transpose(0, 2, 1, 3).reshape(T, LANE))
    return jnp.stack(tabs)


def _inproj_b(x2, norm_w, w, T, tm):
    M, D = x2.shape
    N = w.shape[1]
    hd = DIL_HEAD_DIM
    heads_per_group = 3 * DIL_HEADS
    n_tabs = 3 * len(DIL_GROUPS)
    gate_w = N - len(DIL_GROUPS) * heads_per_group * hd - MEM_WIDTH
    tpb = T // tm
    qkv_specs = [pl.BlockSpec((heads_per_group, tm // r, r * hd), lambda i: (0, i, 0))
                 for _, r in DIL_GROUPS]
    qkv_shapes = [jax.ShapeDtypeStruct((heads_per_group, M // r, r * hd), BF16)
                  for _, r in DIL_GROUPS]
    return pl.pallas_call(
        _inproj_b_kernel,
        grid=(M // tm,),
        in_specs=[
            pl.BlockSpec((tm, D), lambda i: (i, 0)),
            pl.BlockSpec((1, D), lambda i: (0, 0)),
            pl.BlockSpec((D, N), lambda i: (0, 0), pipeline_mode=pl.Buffered(1)),
            pl.BlockSpec((n_tabs, tm, LANE), lambda i: (0, i % tpb, 0)),
        ],
        out_specs=qkv_specs + [
            pl.BlockSpec((tm, MEM_WIDTH), lambda i: (i, 0)),
            pl.BlockSpec((tm, gate_w), lambda i: (i, 0)),
        ],
        out_shape=qkv_shapes + [
            jax.ShapeDtypeStruct((M, MEM_WIDTH), BF16),
            jax.ShapeDtypeStruct((M, gate_w), BF16),
        ],
        scratch_shapes=[pltpu.VMEM((D // LANE, tm, LANE), F32)],
        compiler_params=_params(1),
        name="inproj_dil",
    )(x2, norm_w.reshape(1, D), w, _rope_tables(T, tm))


def _gla_kernel(q_ref, k_ref, v_ref, gl_ref, gate_ref, wgu_ref, bg_ref, nw_ref, o_ref):
    T = q_ref.shape[1]
    C = GLA_CHUNK
    U = GLA_UNROLL
    R = U * C
    row = lax.broadcasted_iota(jnp.int32, (R, R), 0)
    col = lax.broadcasted_iota(jnp.int32, (R, R), 1)
    shift = C.bit_length() - 1
    in_chunk_causal = ((row >> shift) == (col >> shift)) & (row >= col)
    tri = jnp.where(in_chunk_causal, 1.0, 0.0).astype(BF16)
    state = [jnp.zeros((v_ref.shape[-1], q_ref.shape[-1]), F32)]

    def gate_logits(i, _):
        rows = slice(i * R, (i + 1) * R)
        z = _dot(gl_ref[0, rows, :], wgu_ref[...]) + bg_ref[...]
        la = (jnp.minimum(z, 0.0) * (LOG2_E / GLA_TAU)
              - jnp.log2(1.0 + jnp.exp2(jnp.abs(z) * -LOG2_E)) * (1.0 / GLA_TAU))
        hi = la.astype(BF16)
        rem = la - hi.astype(F32)
        mid = rem.astype(BF16)
        lo = (rem - mid.astype(F32)).astype(BF16)
        return rows, hi, mid, lo

    def decays(i, carry):
        rows, hi, mid, lo = carry
        b = _dot(tri, hi) + _dot(tri, mid) + _dot(tri, lo)
        b_last = [b[c * C + C - 1:(c + 1) * C, :] for c in range(U)]
        bl_rows = jnp.concatenate([jnp.broadcast_to(bl, (C, bl.shape[-1])) for bl in b_last], axis=0)
        qc = q_ref[0, rows, :].astype(F32) * (GLA_DK ** -0.5)
        kc = k_ref[0, rows, :].astype(F32)
        q_in = (qc * jnp.exp2(b)).astype(BF16)
        k_in = (kc * jnp.exp2(-b)).astype(BF16)
        k_out = (kc * jnp.exp2(bl_rows - b)).astype(BF16)
        return rows, q_in, k_in, k_out, [jnp.exp2(bl) for bl in b_last]

    def intra(i, carry):
        rows, q_in, k_in, k_out, dec = carry
        vc = v_ref[0, rows, :]
        a = _dot_nt(q_in, k_in)
        ds = [_dot_tn(vc[c * C:(c + 1) * C], k_out[c * C:(c + 1) * C]) for c in range(U)]
        o_intra = _dot(jnp.where(in_chunk_causal, a, 0.0).astype(BF16), vc)
        return rows, q_in, o_intra, ds, dec

    def inter(i, carry):
        rows, q_in, o_intra, ds, dec = carry
        st = state[0]
        o_inter = []
        for c in range(U):
            o_inter.append(_dot_nt(q_in[c * C:(c + 1) * C], st.astype(BF16)))
            st = st * dec[c] + ds[c]
        state[0] = st
        o = o_intra + jnp.concatenate(o_inter, axis=0)
        ms = jnp.sum(o * o, axis=-1, keepdims=True) * (1.0 / GLA_DV)
        y = o * lax.rsqrt(ms + NORM_EPS) * nw_ref[...]
        g = gate_ref[0, rows, :].astype(F32)
        o_ref[0, rows, :] = (y * _silu(g)).astype(BF16)
        return None

    _software_pipeline(T // R, (gate_logits, decays, intra, inter))


def _gla(q, k, v, glow, gate, wgu, bg, nw, B, T):
    kw, vw = GLA_DK_PAD, GLA_DV_PAD
    q3 = q.reshape(B, T, GLA_HEADS * kw)
    k3 = k.reshape(B, T, GLA_HEADS * kw)
    v3 = v.reshape(B, T, GLA_HEADS * vw)
    g3 = glow.reshape(B, T, LANE)
    gate3 = gate.reshape(B, T, gate.shape[-1])
    return pl.pallas_call(
        _gla_kernel,
        grid=(B, GLA_HEADS),
        in_specs=[
            pl.BlockSpec((1, T, kw), lambda b, h: (b, 0, h)),
            pl.BlockSpec((1, T, kw), lambda b, h: (b, 0, h)),
            pl.BlockSpec((1, T, vw), lambda b, h: (b, 0, h)),
            pl.BlockSpec((1, T, LANE), lambda b, h: (b, 0, 0)),
            pl.BlockSpec((1, T, vw), lambda b, h: (b, 0, h)),
            pl.BlockSpec((LANE, kw), lambda b, h: (0, h)),
            pl.BlockSpec((1, kw), lambda b, h: (0, h)),
            pl.BlockSpec((1, vw), lambda b, h: (0, 0)),
        ],
        out_specs=pl.BlockSpec((1, T, vw), lambda b, h: (b, 0, h)),
        out_shape=jax.ShapeDtypeStruct((B, T, GLA_HEADS * vw), BF16),
        compiler_params=_params(2),
        name="gla",
    )(q3, k3, v3, g3, gate3, wgu, bg, nw)


def _dil_kernel(q0, k0, v0, q1, k1, v1, q2, k2, v2, gate_ref, o_ref, o_scr, l_scr):
    T = o_ref.shape[1]
    n = DIL_BLOCK
    hd = DIL_HEAD_DIM
    qi = lax.broadcasted_iota(jnp.int32, (n, 2 * n), 0)
    ki = lax.broadcasted_iota(jnp.int32, (n, 2 * n), 1)
    band = (ki >= qi) & (ki <= qi + n)
    causal = (lax.broadcasted_iota(jnp.int32, (n, n), 1)
              <= lax.broadcasted_iota(jnp.int32, (n, n), 0))
    groups = ((q0, k0, v0), (q1, k1, v1), (q2, k2, v2))

    dilated = [g for g, (_, r) in enumerate(DIL_GROUPS) if r > 1]
    dense = [g for g, (_, r) in enumerate(DIL_GROUPS) if r == 1]
    assert len(dense) == 1 and o_scr.shape[0] == len(dilated)

    tasks = []
    for g in dilated + dense:
        r = DIL_GROUPS[g][1]
        tasks += [(g, p, j) for j in range(T // r // n) for p in range(r)]
    batches = [tasks[t:t + DIL_BATCH] for t in range(0, len(tasks), DIL_BATCH)]

    def scores(i, _):
        out = []
        for g, p, j in batches[i]:
            qr, kr, vr = groups[g]
            cols = slice(p * hd, (p + 1) * hd)
            win = slice(0, n) if j == 0 else slice((j - 1) * n, (j + 1) * n)
            s = _dot_nt(qr[0, 0, j * n:(j + 1) * n, cols], kr[0, 0, win, cols])
            out.append((jnp.where(causal if j == 0 else band, s, -jnp.inf), vr[0, 0, win, cols]))
        return out

    def row_max(i, carry):
        return [(s, vw, jnp.max(s, axis=-1, keepdims=True)) for s, vw in carry]

    def exponentials(i, carry):
        out = []
        for s, vw, m in carry:
            e = jnp.exp2(s - m)
            out.append((e.astype(BF16), vw, m, jnp.sum(e, axis=-1, keepdims=True)))
        return out

    def weighted_values(i, carry):
        for (g, p, j), (e, vw, m, den) in zip(batches[i], carry):
            o = _dot(e, vw) * (1.0 / den)
            lse = jnp.broadcast_to(m + jnp.log2(den), o.shape)
            r = DIL_GROUPS[g][1]
            if r > 1:
                slot = dilated.index(g)
                dst = pl.ds(j * n * r + p, n, stride=r)
                o_scr[slot, dst, :] = o
                l_scr[slot, dst, :] = lse
            else:
                rows = slice(j * n, (j + 1) * n)
                ls = [lse] + [l_scr[slot, rows, :] for slot in range(len(dilated))]
                os_ = [o] + [o_scr[slot, rows, :] for slot in range(len(dilated))]
                mx = functools.reduce(jnp.maximum, ls)
                ws = [jnp.exp2(l - mx) for l in ls]
                num = functools.reduce(lambda a, b: a + b, [w * ov for w, ov in zip(ws, os_)])
                mixed = num / functools.reduce(lambda a, b: a + b, ws)
                gt = gate_ref[0, rows, :].astype(F32)
                o_ref[0, rows, :] = (mixed * _silu(gt)).astype(BF16)
        return None

    _software_pipeline(len(batches), (scores, row_max, exponentials, weighted_values))


def _dilated(qkv_groups, gate, B, T):
    hd = DIL_HEAD_DIM
    in_specs, args = [], []
    for (_, r), qkv in zip(DIL_GROUPS, qkv_groups):
        view = qkv.reshape(qkv.shape[0], B, T // r, r * hd)
        for s in range(3):
            in_specs.append(pl.BlockSpec((1, 1, T // r, r * hd),
                                         lambda b, h, s=s: (s * DIL_HEADS + h, b, 0, 0)))
            args.append(view)
    in_specs.append(pl.BlockSpec((1, T, hd), lambda b, h: (b, 0, h)))
    args.append(gate.reshape(B, T, gate.shape[-1]))
    n_groups = sum(r > 1 for _, r in DIL_GROUPS)
    return pl.pallas_call(
        _dil_kernel,
        grid=(B, DIL_HEADS),
        in_specs=in_specs,
        out_specs=pl.BlockSpec((1, T, hd), lambda b, h: (b, 0, h)),
        out_shape=jax.ShapeDtypeStruct((B, T, DIL_HEADS * hd), BF16),
        scratch_shapes=[pltpu.VMEM((n_groups, T, hd), F32), pltpu.VMEM((n_groups, T, hd), F32)],
        compiler_params=_params(2),
        name="dilated",
    )(*args)


def _out_kernel(mix_ref, qm_ref, gm_ref, kst_ref, vst_ref, w_ref, x_ref, fnw_ref, o_ref, *,
                final, mix_pad):
    n_mem = kst_ref.shape[0]
    tm = mix_ref.shape[0]
    rs = OUT_ROWS
    lane = lax.broadcasted_iota(jnp.int32, (n_mem, MEM_WIDTH), 1)
    head_lanes = [(lane >= hh * MEM_HEAD_DIM) & (lane < (hh + 1) * MEM_HEAD_DIM)
                  for hh in range(MEM_HEADS)]
    zero = jnp.zeros((n_mem, MEM_WIDTH), BF16)
    k_heads = jnp.concatenate([jnp.where(msk, kst_ref[...], zero) for msk in head_lanes], axis=0)
    v_heads = [jnp.where(msk, vst_ref[...], zero) for msk in head_lanes]

    def mixer_rows(rows):
        mix = mix_ref[rows, :]
        if mix_pad is None:
            return mix
        heads, width, padded = mix_pad
        return jnp.concatenate([mix[:, hh * padded:hh * padded + width] for hh in range(heads)], axis=1)

    def scores(i, _):
        rows = slice(i * rs, (i + 1) * rs)
        return rows, _dot_nt(qm_ref[rows, :], k_heads)

    def softmax(i, carry):
        rows, s = carry
        ps = []
        for hh in range(MEM_HEADS):
            seg = s[:, hh * n_mem:(hh + 1) * n_mem]
            e = jnp.exp(seg - jnp.max(seg, axis=-1, keepdims=True))
            ps.append((e / jnp.sum(e, axis=-1, keepdims=True)).astype(BF16))
        return rows, ps

    def values(i, carry):
        rows, ps = carry
        mo = None
        for hh, p in enumerate(ps):
            part = _dot(p, v_heads[hh])
            mo = part if mo is None else mo + part
        return rows, (mo * _silu(gm_ref[rows, :].astype(F32))).astype(BF16)

    def project(i, carry):
        rows, bm = carry
        branch = jnp.concatenate([mixer_rows(rows), bm], axis=-1)
        y = _dot(branch, w_ref[...]) + x_ref[rows, :]
        if final:
            y = _rmsnorm(y, fnw_ref[...])
        o_ref[rows, :] = y
        return None

    _software_pipeline(tm // rs, (scores, softmax, values, project))


def _out_proj(mix2, qm2, gate2, kv, layer, w_out, x2, fnw, T, tm, final, mix_pad=None):
    M, D = x2.shape
    wm = mix2.shape[1]
    gate_blk = (gate2.shape[1] - MEM_WIDTH) // MEM_WIDTH
    tpb = T // tm
    kv_spec = lambda which: pl.BlockSpec(
        (None, None, None, kv.shape[3], MEM_WIDTH), lambda i: (layer, which, i // tpb, 0, 0))
    return pl.pallas_call(
        functools.partial(_out_kernel, final=final, mix_pad=mix_pad),
        grid=(M // tm,),
        in_specs=[
            pl.BlockSpec((tm, wm), lambda i: (i, 0)),
            pl.BlockSpec((tm, MEM_WIDTH), lambda i: (i, 0)),
            pl.BlockSpec((tm, MEM_WIDTH), lambda i: (i, gate_blk)),
            kv_spec(0), kv_spec(1),
            pl.BlockSpec(w_out.shape, lambda i: (0, 0)),
            pl.BlockSpec((tm, D), lambda i: (i, 0)),
            pl.BlockSpec((1, D), lambda i: (0, 0)),
        ],
        out_specs=pl.BlockSpec((tm, D), lambda i: (i, 0)),
        out_shape=jax.ShapeDtypeStruct((M, D), F32),
        compiler_params=_params(1),
        name="memattn_outproj",
    )(mix2, qm2, gate2, kv, kv, w_out, x2, fnw.reshape(1, D))


def _pad_heads(w, heads, width, padded, axis=-1):
    axis = axis % w.ndim
    shape = w.shape[:axis] + (heads, width) + w.shape[axis + 1:]
    pads = [(0, 0)] * (w.ndim + 1)
    pads[axis + 1] = (0, padded - width)
    out = jnp.pad(w.reshape(shape), pads)
    return out.reshape(w.shape[:axis] + (heads * padded,) + w.shape[axis + 1:])


def _gla_weights(w_in, w_gate_up, b_gate, gla_norm_w):
    hk = GLA_HEADS * GLA_DK
    mixw = GLA_HEADS * GLA_DV
    c = [0, hk, 2 * hk, 2 * hk + mixw, 2 * hk + mixw + GLA_RANK, 2 * hk + mixw + GLA_RANK + MEM_WIDTH]
    q, k, v = w_in[:, c[0]:c[1]], w_in[:, c[1]:c[2]], w_in[:, c[2]:c[3]]
    gl, qm, gate = w_in[:, c[3]:c[4]], w_in[:, c[4]:c[5]], w_in[:, c[5]:]
    w = jnp.concatenate([q, k, v, gate, qm, jnp.pad(gl, ((0, 0), (0, LANE - GLA_RANK)))],
                        axis=1).astype(BF16)
    widths = (GLA_HEADS * GLA_DK_PAD, GLA_HEADS * GLA_DK_PAD, GLA_HEADS * GLA_DV_PAD, LANE,
              MEM_WIDTH, GLA_HEADS * GLA_DV_PAD + MEM_WIDTH)
    wgu = jnp.pad(_pad_heads(w_gate_up, GLA_HEADS, GLA_DK, GLA_DK_PAD),
                  ((0, LANE - GLA_RANK), (0, 0))).astype(BF16)
    bg = _pad_heads(b_gate.reshape(1, hk), GLA_HEADS, GLA_DK, GLA_DK_PAD)
    nw = jnp.pad(gla_norm_w.reshape(1, GLA_DV), ((0, 0), (0, GLA_DV_PAD - GLA_DV)))
    return w, widths, wgu, bg, nw


def kernel(x, mem, mem_norm_w, norm_w, w_memkv, w_out, w_in_a, w_gate_up, b_gate, gla_norm_w,
           w_in_b, final_norm_w):
    B, T, D = x.shape
    depth = norm_w.shape[0]
    M = B * T
    tm = 512
    kv = _memkv(mem, mem_norm_w, w_memkv)
    x2 = x.reshape(M, D)
    for i in range(depth):
        j = i // 2
        final = i == depth - 1
        if i % 2 == 0:
            w, widths, wgu, bg, nw = _gla_weights(w_in_a[j], w_gate_up[j], b_gate[j], gla_norm_w[j])
            mix_pad = (GLA_HEADS, GLA_DV, GLA_DV_PAD)
            q, k, v, glow, qm, gate = _inproj_a(x2, norm_w[i], w, widths, tm)
            mix = _gla(q, k, v, glow, gate, wgu, bg, nw, B, T)
        else:
            mix_pad = None
            *qkv_groups, qm, gate = _inproj_b(x2, norm_w[i], w_in_b[j].astype(BF16), T, tm)
            mix = _dilated(qkv_groups, gate, B, T)
        x2 = _out_proj(mix.reshape(M, -1), qm, gate, kv, i, w_out[i].astype(BF16), x2, final_norm_w,
                       T, OUT_TILE, final, mix_pad)
    return x2.reshape(B, T, D)
```

```python
import functools

import jax
import jax.numpy as jnp
from jax import lax
from jax.experimental import pallas as pl
from jax.experimental.pallas import tpu as pltpu

F32 = jnp.float32
BF16 = jnp.bfloat16

LANE = 128
NORM_EPS = 1e-6
LOG2_E = 1.4426950408889634
MEM_HEADS = 4
MEM_HEAD_DIM = 64
MEM_WIDTH = MEM_HEADS * MEM_HEAD_DIM
GLA_HEADS = 4
GLA_DK = 96
GLA_DV = 192
GLA_DK_PAD = 128
GLA_DV_PAD = 256
GLA_RANK = 16
GLA_TAU = 16.0
GLA_CHUNK = 64
GLA_UNROLL = 2
GLA_HEADS_PER_STEP = 2
DIL_GROUPS = ((128, 1), (512, 4), (2048, 16))
DIL_HEADS = 6
DIL_HEAD_DIM = 128
DIL_BLOCK = 128
DIL_BATCH = 2
DIL_HEADS_PER_STEP = 2
OUT_ROWS = 256
OUT_TILE = 1024
ROPE_THETA = 500000.0
ROPE_DIM = 32
VMEM_LIMIT = 56 * 1024 * 1024

_NT = (((1,), (1,)), ((), ()))
_TN = (((0,), (0,)), ((), ()))


def _dot(a, b):
    return jnp.dot(a, b, preferred_element_type=F32)


def _dot_nt(a, b):
    return lax.dot_general(a, b, _NT, preferred_element_type=F32)


def _dot_tn(a, b):
    return lax.dot_general(a, b, _TN, preferred_element_type=F32)


def _rmsnorm(x, w):
    ms = jnp.mean(x * x, axis=-1, keepdims=True)
    return x * lax.rsqrt(ms + NORM_EPS) * w


def _silu(g):
    return g * (0.5 + 0.5 * jnp.tanh(0.5 * g))


def _software_pipeline(n_items, stages):
    carries = [None] * n_items
    for t in range(n_items + len(stages) - 1):
        for k, stage in enumerate(stages):
            i = t - k
            if 0 <= i < n_items:
                carries[i] = stage(i, carries[i])


def _params(n_parallel):
    return pltpu.CompilerParams(
        dimension_semantics=("parallel",) * n_parallel,
        vmem_limit_bytes=VMEM_LIMIT)


def _memkv_kernel(mem_ref, nw_ref, w_ref, kv_ref, *, depth):
    h = _rmsnorm(mem_ref[0], nw_ref[...]).astype(BF16)
    kv = _dot(h, w_ref[...])
    for l in range(depth):
        base = l * 2 * MEM_WIDTH
        k = kv[:, base:base + MEM_WIDTH] * (MEM_HEAD_DIM ** -0.5)
        kv_ref[l, 0, 0] = k.astype(BF16)
        kv_ref[l, 1, 0] = kv[:, base + MEM_WIDTH:base + 2 * MEM_WIDTH].astype(BF16)


def _memkv(mem, mem_norm_w, w_memkv):
    B, n_mem, D = mem.shape
    depth = w_memkv.shape[0]
    w = jnp.transpose(w_memkv, (1, 0, 2)).reshape(D, depth * 2 * MEM_WIDTH).astype(BF16)
    return pl.pallas_call(
        functools.partial(_memkv_kernel, depth=depth),
        grid=(B,),
        in_specs=[
            pl.BlockSpec((1, n_mem, D), lambda b: (b, 0, 0)),
            pl.BlockSpec((1, D), lambda b: (0, 0)),
            pl.BlockSpec((D, depth * 2 * MEM_WIDTH), lambda b: (0, 0)),
        ],
        out_specs=pl.BlockSpec((depth, 2, 1, n_mem, MEM_WIDTH), lambda b: (0, 0, b, 0, 0)),
        out_shape=jax.ShapeDtypeStruct((depth, 2, B, n_mem, MEM_WIDTH), BF16),
        compiler_params=_params(1),
        name="memkv",
    )(mem, mem_norm_w.reshape(1, D), w)


def _inproj_a_kernel(x_ref, nw_ref, w_ref, q_ref, k_ref, v_ref, gl_ref, qm_ref, gate_ref):
    h = _rmsnorm(x_ref[...], nw_ref[...]).astype(BF16)
    tm = h.shape[0]
    hk = GLA_HEADS * GLA_DK
    mixw = GLA_HEADS * GLA_DV

    def pad_heads(val, width, padded):
        zeros = jnp.zeros((tm, padded - width), val.dtype)
        pieces = []
        for hh in range(GLA_HEADS):
            pieces += [val[:, hh * width:(hh + 1) * width], zeros]
        return jnp.concatenate(pieces, axis=1).astype(BF16)

    c0 = 0
    qk = _dot(h, w_ref[:, c0:c0 + 2 * hk])
    q_ref[...] = pad_heads(qk[:, :hk], GLA_DK, GLA_DK_PAD)
    k_ref[...] = pad_heads(qk[:, hk:], GLA_DK, GLA_DK_PAD)
    c0 += 2 * hk
    v_ref[...] = pad_heads(_dot(h, w_ref[:, c0:c0 + mixw]), GLA_DV, GLA_DV_PAD)
    c0 += mixw
    gate = _dot(h, w_ref[:, c0:c0 + mixw + MEM_WIDTH])
    gate_ref[:, :GLA_HEADS * GLA_DV_PAD] = pad_heads(gate[:, :mixw], GLA_DV, GLA_DV_PAD)
    gate_ref[:, GLA_HEADS * GLA_DV_PAD:] = gate[:, mixw:].astype(BF16)
    c0 += mixw + MEM_WIDTH
    qm_ref[...] = _dot(h, w_ref[:, c0:c0 + MEM_WIDTH]).astype(BF16)
    c0 += MEM_WIDTH
    gl_ref[...] = _dot(h, w_ref[:, c0:]).astype(BF16)


def _inproj_a(x2, norm_w, w, widths, tm):
    M, D = x2.shape
    N = w.shape[1]
    return pl.pallas_call(
        _inproj_a_kernel,
        grid=(M // tm,),
        in_specs=[
            pl.BlockSpec((tm, D), lambda i: (i, 0)),
            pl.BlockSpec((1, D), lambda i: (0, 0)),
            pl.BlockSpec((D, N), lambda i: (0, 0)),
        ],
        out_specs=[pl.BlockSpec((tm, cw), lambda i: (i, 0)) for cw in widths],
        out_shape=[jax.ShapeDtypeStruct((M, cw), BF16) for cw in widths],
        compiler_params=_params(1),
        name="inproj_gla",
    )(x2, norm_w.reshape(1, D), w)


def _inproj_b_kernel(x_ref, nw_ref, w_ref, rope_ref, qkv0_ref, qkv1_ref, qkv2_ref, qm_ref, gate_ref,
                     h_scr):
    hf = _rmsnorm(x_ref[...], nw_ref[...])
    h = hf.astype(BF16)
    tm, D = hf.shape
    hd = DIL_HEAD_DIM
    hw = DIL_HEADS * hd
    n_lt = D // LANE
    for lt in range(n_lt):
        h_scr[lt] = hf[:, lt * LANE:(lt + 1) * LANE]
    group_refs = (qkv0_ref, qkv1_ref, qkv2_ref)
    for g, (_, r) in enumerate(DIL_GROUPS):
        out_ref = group_refs[g]
        rows_p = tm // r
        if r == 1:
            hg = h
        else:
            hg = jnp.concatenate(
                [jnp.concatenate([h_scr[lt, pl.ds(p, rows_p, stride=r), :] for lt in range(n_lt)],
                                 axis=1).astype(BF16) for p in range(r)], axis=0)
        cos, sin_up, sin_dn = rope_ref[3 * g], rope_ref[3 * g + 1], rope_ref[3 * g + 2]
        for s in range(3):
            ci = g * 3 + s
            acc = _dot(hg, w_ref[:, ci * hw:(ci + 1) * hw])
            for hh in range(DIL_HEADS):
                slab = acc[:, hh * hd:(hh + 1) * hd]
                if s < 2:
                    slab = (slab * cos + pltpu.roll(slab, ROPE_DIM // 2, 1) * sin_up
                            + pltpu.roll(slab, LANE - ROPE_DIM // 2, 1) * sin_dn)
                if s == 0:
                    slab = slab * (hd ** -0.5 * LOG2_E)
                slab = slab.astype(BF16)
                if r == 1:
                    out_ref[s * DIL_HEADS + hh] = slab
                else:
                    for p in range(r):
                        out_ref[s * DIL_HEADS + hh, :, p * hd:(p + 1) * hd] = (
                            slab[p * rows_p:(p + 1) * rows_p, :])
    c0 = len(DIL_GROUPS) * 3 * hw
    qm_ref[...] = _dot(h, w_ref[:, c0:c0 + MEM_WIDTH]).astype(BF16)
    c0 += MEM_WIDTH
    gate_ref[...] = _dot(h, w_ref[:, c0:]).astype(BF16)


def _rope_tables(T, tm):
    half = ROPE_DIM // 2
    inv = ROPE_THETA ** (-jnp.arange(half, dtype=F32) / half)
    ang = jnp.arange(T).astype(F32)[:, None] * inv[None, :]
    cos, sin = jnp.cos(ang), jnp.sin(ang)
    ones = jnp.ones((T, LANE - ROPE_DIM), F32)
    zeros = jnp.zeros((T, LANE - ROPE_DIM), F32)
    zh = jnp.zeros((T, half), F32)
    base = [jnp.concatenate([cos, cos, ones], axis=-1),
            jnp.concatenate([zh, sin, zeros], axis=-1),
            jnp.concatenate([-sin, zh, zeros], axis=-1)]
    tabs = []
    for _, r in DIL_GROUPS:
        for t in base:
            tabs.append(t.reshape(T // tm, tm // r, r, LANE).transpose(0, 2, 1, 3).reshape(T, LANE))
    return jnp.stack(tabs)


def _inproj_b(x2, norm_w, w, T, tm):
    M, D = x2.shape
    N = w.shape[1]
    hd = DIL_HEAD_DIM
    heads_per_group = 3 * DIL_HEADS
    n_tabs = 3 * len(DIL_GROUPS)
    gate_w = N - len(DIL_GROUPS) * heads_per_group * hd - MEM_WIDTH
    tpb = T // tm
    qkv_specs = [pl.BlockSpec((heads_per_group, tm // r, r * hd), lambda i: (0, i, 0))
                 for _, r in DIL_GROUPS]
    qkv_shapes = [jax.ShapeDtypeStruct((heads_per_group, M // r, r * hd), BF16)
                  for _, r in DIL_GROUPS]
    return pl.pallas_call(
        _inproj_b_kernel,
        grid=(M // tm,),
        in_specs=[
            pl.BlockSpec((tm, D), lambda i: (i, 0)),
            pl.BlockSpec((1, D), lambda i: (0, 0)),
            pl.BlockSpec((D, N), lambda i: (0, 0), pipeline_mode=pl.Buffered(1)),
            pl.BlockSpec((n_tabs, tm, LANE), lambda i: (0, i % tpb, 0)),
        ],
        out_specs=qkv_specs + [
            pl.BlockSpec((tm, MEM_WIDTH), lambda i: (i, 0)),
            pl.BlockSpec((tm, gate_w), lambda i: (i, 0)),
        ],
        out_shape=qkv_shapes + [
            jax.ShapeDtypeStruct((M, MEM_WIDTH), BF16),
            jax.ShapeDtypeStruct((M, gate_w), BF16),
        ],
        scratch_shapes=[pltpu.VMEM((D // LANE, tm, LANE), F32)],
        compiler_params=_params(1),
        name="inproj_dil",
    )(x2, norm_w.reshape(1, D), w, _rope_tables(T, tm))


def _gla_kernel(q_ref, k_ref, v_ref, gl_ref, gate_ref, wgu_ref, bg_ref, nw_ref, o_ref):
    T = q_ref.shape[1]
    C = GLA_CHUNK
    U = GLA_UNROLL
    R = U * C
    kw, vw = GLA_DK_PAD, GLA_DV_PAD
    heads = q_ref.shape[2] // kw
    row = lax.broadcasted_iota(jnp.int32, (R, R), 0)
    col = lax.broadcasted_iota(jnp.int32, (R, R), 1)
    shift = C.bit_length() - 1
    in_chunk_causal = ((row >> shift) == (col >> shift)) & (row >= col)
    tri = jnp.where(in_chunk_causal, 1.0, 0.0).astype(BF16)
    tri3 = jnp.concatenate([tri, tri, tri], axis=1)
    state = [jnp.zeros((vw, kw), F32) for _ in range(heads)]

    def gate_logits(i, _):
        hh, ci = i % heads, i // heads
        rows = slice(ci * R, (ci + 1) * R)
        kl = slice(hh * kw, (hh + 1) * kw)
        z = _dot(gl_ref[0, rows, :], wgu_ref[:, kl]) + bg_ref[:, kl]
        la = (jnp.minimum(z, 0.0) * (LOG2_E / GLA_TAU)
              - jnp.log2(1.0 + jnp.exp2(jnp.abs(z) * -LOG2_E)) * (1.0 / GLA_TAU))
        hi = la.astype(BF16)
        rem = la - hi.astype(F32)
        mid = rem.astype(BF16)
        lo = (rem - mid.astype(F32)).astype(BF16)
        return hh, rows, kl, jnp.concatenate([hi, mid, lo], axis=0)

    def decays(i, carry):
        hh, rows, kl, pieces = carry
        b = _dot(tri3, pieces)
        b_last = [b[c * C + C - 1:(c + 1) * C, :] for c in range(U)]
        bl_rows = jnp.concatenate([jnp.broadcast_to(bl, (C, bl.shape[-1])) for bl in b_last], axis=0)
        qc = q_ref[0, rows, kl].astype(F32) * (GLA_DK ** -0.5)
        kc = k_ref[0, rows, kl].astype(F32)
        q_in = (qc * jnp.exp2(b)).astype(BF16)
        k_in = (kc * jnp.exp2(-b)).astype(BF16)
        k_out = (kc * jnp.exp2(bl_rows - b)).astype(BF16)
        return hh, rows, q_in, k_in, k_out, [jnp.exp2(bl) for bl in b_last]

    def intra(i, carry):
        hh, rows, q_in, k_in, k_out, dec = carry
        vc = v_ref[0, rows, hh * vw:(hh + 1) * vw]
        a = _dot_nt(q_in, k_in)
        ds = [_dot_tn(vc[c * C:(c + 1) * C], k_out[c * C:(c + 1) * C]) for c in range(U)]
        o_intra = _dot(jnp.where(in_chunk_causal, a, 0.0).astype(BF16), vc)
        return hh, rows, q_in, o_intra, ds, dec

    def inter(i, carry):
        hh, rows, q_in, o_intra, ds, dec = carry
        vl = slice(hh * vw, (hh + 1) * vw)
        st = state[hh]
        o_inter = []
        for c in range(U):
            o_inter.append(_dot_nt(q_in[c * C:(c + 1) * C], st.astype(BF16)))
            st = st * dec[c] + ds[c]
        state[hh] = st
        o = o_intra + jnp.concatenate(o_inter, axis=0)
        ms = jnp.sum(o * o, axis=-1, keepdims=True) * (1.0 / GLA_DV)
        y = o * lax.rsqrt(ms + NORM_EPS) * nw_ref[...]
        g = gate_ref[0, rows, vl].astype(F32)
        o_ref[0, rows, vl] = (y * _silu(g)).astype(BF16)
        return None

    _software_pipeline(heads * (T // R), (gate_logits, decays, intra, inter))


def _gla(q, k, v, glow, gate, wgu, bg, nw, B, T):
    hps = GLA_HEADS_PER_STEP
    kw, vw = hps * GLA_DK_PAD, hps * GLA_DV_PAD
    q3 = q.reshape(B, T, q.shape[-1])
    k3 = k.reshape(B, T, k.shape[-1])
    v3 = v.reshape(B, T, v.shape[-1])
    g3 = glow.reshape(B, T, LANE)
    gate3 = gate.reshape(B, T, gate.shape[-1])
    return pl.pallas_call(
        _gla_kernel,
        grid=(B, GLA_HEADS // hps),
        in_specs=[
            pl.BlockSpec((1, T, kw), lambda b, h: (b, 0, h)),
            pl.BlockSpec((1, T, kw), lambda b, h: (b, 0, h)),
            pl.BlockSpec((1, T, vw), lambda b, h: (b, 0, h)),
            pl.BlockSpec((1, T, LANE), lambda b, h: (b, 0, 0)),
            pl.BlockSpec((1, T, vw), lambda b, h: (b, 0, h)),
            pl.BlockSpec((LANE, kw), lambda b, h: (0, h)),
            pl.BlockSpec((1, kw), lambda b, h: (0, h)),
            pl.BlockSpec((1, GLA_DV_PAD), lambda b, h: (0, 0)),
        ],
        out_specs=pl.BlockSpec((1, T, vw), lambda b, h: (b, 0, h)),
        out_shape=jax.ShapeDtypeStruct((B, T, GLA_HEADS * GLA_DV_PAD), BF16),
        compiler_params=_params(2),
        name="gla",
    )(q3, k3, v3, g3, gate3, wgu, bg, nw)


def _dil_kernel(q0, k0, v0, q1, k1, v1, q2, k2, v2, gate_ref, o_ref, o_scr, l_scr):
    T = o_ref.shape[1]
    n = DIL_BLOCK
    hd = DIL_HEAD_DIM
    heads = q0.shape[0]
    qi = lax.broadcasted_iota(jnp.int32, (n, 2 * n), 0)
    ki = lax.broadcasted_iota(jnp.int32, (n, 2 * n), 1)
    band = (ki >= qi) & (ki <= qi + n)
    causal = (lax.broadcasted_iota(jnp.int32, (n, n), 1)
              <= lax.broadcasted_iota(jnp.int32, (n, n), 0))
    groups = ((q0, k0, v0), (q1, k1, v1), (q2, k2, v2))

    dilated = [g for g, (_, r) in enumerate(DIL_GROUPS) if r > 1]
    dense = [g for g, (_, r) in enumerate(DIL_GROUPS) if r == 1]
    assert len(dense) == 1 and o_scr.shape[0] == heads * len(dilated)

    tasks = []
    for hh in range(heads):
        for g in dilated + dense:
            r = DIL_GROUPS[g][1]
            tasks += [(hh, g, p, j) for j in range(T // r // n) for p in range(r)]
    batches = [tasks[t:t + DIL_BATCH] for t in range(0, len(tasks), DIL_BATCH)]

    def scores(i, _):
        out = []
        for hh, g, p, j in batches[i]:
            qr, kr, vr = groups[g]
            cols = slice(p * hd, (p + 1) * hd)
            win = slice(0, n) if j == 0 else slice((j - 1) * n, (j + 1) * n)
            s = _dot_nt(qr[hh, 0, j * n:(j + 1) * n, cols], kr[hh, 0, win, cols])
            out.append((jnp.where(causal if j == 0 else band, s, -jnp.inf), vr[hh, 0, win, cols]))
        return out

    def row_max(i, carry):
        return [(s, vw, jnp.max(s, axis=-1, keepdims=True)) for s, vw in carry]

    def exponentials(i, carry):
        out = []
        for s, vw, m in carry:
            e = jnp.exp2(s - m)
            out.append((e.astype(BF16), vw, m, jnp.sum(e, axis=-1, keepdims=True)))
        return out

    def weighted_values(i, carry):
        for (hh, g, p, j), (e, vw, m, den) in zip(batches[i], carry):
            o = _dot(e, vw) * (1.0 / den)
            lse = jnp.broadcast_to(m + jnp.log2(den), o.shape)
            r = DIL_GROUPS[g][1]
            if r > 1:
                slot = hh * len(dilated) + dilated.index(g)
                dst = pl.ds(j * n * r + p, n, stride=r)
                o_scr[slot, dst, :] = o
                l_scr[slot, dst, :] = lse
            else:
                rows = slice(j * n, (j + 1) * n)
                slots = range(hh * len(dilated), (hh + 1) * len(dilated))
                ls = [lse] + [l_scr[slot, rows, :] for slot in slots]
                os_ = [o] + [o_scr[slot, rows, :] for slot in slots]
                mx = functools.reduce(jnp.maximum, ls)
                ws = [jnp.exp2(l - mx) for l in ls]
                num = functools.reduce(lambda a, b: a + b, [w * ov for w, ov in zip(ws, os_)])
                mixed = num / functools.reduce(lambda a, b: a + b, ws)
                cols = slice(hh * hd, (hh + 1) * hd)
                gt = gate_ref[0, rows, cols].astype(F32)
                o_ref[0, rows, cols] = (mixed * _silu(gt)).astype(BF16)
        return None

    _software_pipeline(len(batches), (scores, row_max, exponentials, weighted_values))


def _dilated(qkv_groups, gate, B, T):
    hd = DIL_HEAD_DIM
    hps = DIL_HEADS_PER_STEP
    steps = DIL_HEADS // hps
    in_specs, args = [], []
    for (_, r), qkv in zip(DIL_GROUPS, qkv_groups):
        view = qkv.reshape(qkv.shape[0], B, T // r, r * hd)
        for s in range(3):
            in_specs.append(pl.BlockSpec((hps, 1, T // r, r * hd),
                                         lambda b, h, s=s: (s * steps + h, b, 0, 0)))
            args.append(view)
    in_specs.append(pl.BlockSpec((1, T, hps * hd), lambda b, h: (b, 0, h)))
    args.append(gate.reshape(B, T, gate.shape[-1]))
    n_slots = hps * sum(r > 1 for _, r in DIL_GROUPS)
    return pl.pallas_call(
        _dil_kernel,
        grid=(B, steps),
        in_specs=in_specs,
        out_specs=pl.BlockSpec((1, T, hps * hd), lambda b, h: (b, 0, h)),
        out_shape=jax.ShapeDtypeStruct((B, T, DIL_HEADS * hd), BF16),
        scratch_shapes=[pltpu.VMEM((n_slots, T, hd), F32), pltpu.VMEM((n_slots, T, hd), F32)],
        compiler_params=_params(2),
        name="dilated",
    )(*args)


def _out_kernel(mix_ref, qm_ref, gm_ref, kst_ref, vst_ref, w_ref, x_ref, fnw_ref, o_ref, *,
                final, mix_pad):
    n_mem = kst_ref.shape[0]
    tm = mix_ref.shape[0]
    rs = OUT_ROWS
    lane = lax.broadcasted_iota(jnp.int32, (n_mem, MEM_WIDTH), 1)
    head_lanes = [(lane >= hh * MEM_HEAD_DIM) & (lane < (hh + 1) * MEM_HEAD_DIM)
                  for hh in range(MEM_HEADS)]
    zero = jnp.zeros((n_mem, MEM_WIDTH), BF16)
    k_heads = jnp.concatenate([jnp.where(msk, kst_ref[...], zero) for msk in head_lanes], axis=0)
    v_heads = [jnp.where(msk, vst_ref[...], zero) for msk in head_lanes]

    def mixer_rows(rows):
        mix = mix_ref[rows, :]
        if mix_pad is None:
            return mix
        heads, width, padded = mix_pad
        return jnp.concatenate([mix[:, hh * padded:hh * padded + width] for hh in range(heads)], axis=1)

    def scores(i, _):
        rows = slice(i * rs, (i + 1) * rs)
        return rows, _dot_nt(qm_ref[rows, :], k_heads)

    def softmax(i, carry):
        rows, s = carry
        ps = []
        for hh in range(MEM_HEADS):
            seg = s[:, hh * n_mem:(hh + 1) * n_mem]
            e = jnp.exp(seg - jnp.max(seg, axis=-1, keepdims=True))
            ps.append((e / jnp.sum(e, axis=-1, keepdims=True)).astype(BF16))
        return rows, ps

    def values(i, carry):
        rows, ps = carry
        mo = None
        for hh, p in enumerate(ps):
            part = _dot(p, v_heads[hh])
            mo = part if mo is None else mo + part
        return rows, (mo * _silu(gm_ref[rows, :].astype(F32))).astype(BF16)

    def project(i, carry):
        rows, bm = carry
        branch = jnp.concatenate([mixer_rows(rows), bm], axis=-1)
        y = _dot(branch, w_ref[...]) + x_ref[rows, :]
        if final:
            y = _rmsnorm(y, fnw_ref[...])
        o_ref[rows, :] = y
        return None

    _software_pipeline(tm // rs, (scores, softmax, values, project))


def _out_proj(mix2, qm2, gate2, kv, layer, w_out, x2, fnw, T, tm, final, mix_pad=None):
    M, D = x2.shape
    wm = mix2.shape[1]
    gate_blk = (gate2.shape[1] - MEM_WIDTH) // MEM_WIDTH
    tpb = T // tm
    kv_spec = lambda which: pl.BlockSpec(
        (None, None, None, kv.shape[3], MEM_WIDTH), lambda i: (layer, which, i // tpb, 0, 0))
    return pl.pallas_call(
        functools.partial(_out_kernel, final=final, mix_pad=mix_pad),
        grid=(M // tm,),
        in_specs=[
            pl.BlockSpec((tm, wm), lambda i: (i, 0)),
            pl.BlockSpec((tm, MEM_WIDTH), lambda i: (i, 0)),
            pl.BlockSpec((tm, MEM_WIDTH), lambda i: (i, gate_blk)),
            kv_spec(0), kv_spec(1),
            pl.BlockSpec(w_out.shape, lambda i: (0, 0)),
            pl.BlockSpec((tm, D), lambda i: (i, 0)),
            pl.BlockSpec((1, D), lambda i: (0, 0)),
        ],
        out_specs=pl.BlockSpec((tm, D), lambda i: (i, 0)),
        out_shape=jax.ShapeDtypeStruct((M, D), F32),
        compiler_params=_params(1),
        name="memattn_outproj",
    )(mix2, qm2, gate2, kv, kv, w_out, x2, fnw.reshape(1, D))


def _pad_heads(w, heads, width, padded, axis=-1):
    axis = axis % w.ndim
    shape = w.shape[:axis] + (heads, width) + w.shape[axis + 1:]
    pads = [(0, 0)] * (w.ndim + 1)
    pads[axis + 1] = (0, padded - width)
    out = jnp.pad(w.reshape(shape), pads)
    return out.reshape(w.shape[:axis] + (heads * padded,) + w.shape[axis + 1:])


def _gla_weights(w_in, w_gate_up, b_gate, gla_norm_w):
    hk = GLA_HEADS * GLA_DK
    mixw = GLA_HEADS * GLA_DV
    c = [0, hk, 2 * hk, 2 * hk + mixw, 2 * hk + mixw + GLA_RANK, 2 * hk + mixw + GLA_RANK + MEM_WIDTH]
    q, k, v = w_in[:, c[0]:c[1]], w_in[:, c[1]:c[2]], w_in[:, c[2]:c[3]]
    gl, qm, gate = w_in[:, c[3]:c[4]], w_in[:, c[4]:c[5]], w_in[:, c[5]:]
    w = jnp.concatenate([q, k, v, gate, qm, jnp.pad(gl, ((0, 0), (0, LANE - GLA_RANK)))],
                        axis=1).astype(BF16)
    widths = (GLA_HEADS * GLA_DK_PAD, GLA_HEADS * GLA_DK_PAD, GLA_HEADS * GLA_DV_PAD, LANE,
              MEM_WIDTH, GLA_HEADS * GLA_DV_PAD + MEM_WIDTH)
    wgu = jnp.pad(_pad_heads(w_gate_up, GLA_HEADS, GLA_DK, GLA_DK_PAD),
                  ((0, LANE - GLA_RANK), (0, 0))).astype(BF16)
    bg = _pad_heads(b_gate.reshape(1, hk), GLA_HEADS, GLA_DK, GLA_DK_PAD)
    nw = jnp.pad(gla_norm_w.reshape(1, GLA_DV), ((0, 0), (0, GLA_DV_PAD - GLA_DV)))
    return w, widths, wgu, bg, nw


def kernel(x, mem, mem_norm_w, norm_w, w_memkv, w_out, w_in_a, w_gate_up, b_gate, gla_norm_w,
           w_in_b, final_norm_w):
    B, T, D = x.shape
    depth = norm_w.shape[0]
    M = B * T
    tm = 512
    kv = _memkv(mem, mem_norm_w, w_memkv)
    x2 = x.reshape(M, D)
    for i in range(depth):
        j = i // 2
        final = i == depth - 1
        if i % 2 == 0:
            w, widths, wgu, bg, nw = _gla_weights(w_in_a[j], w_gate_up[j], b_gate[j], gla_norm_w[j])
            mix_pad = (GLA_HEADS, GLA_DV, GLA_DV_PAD)
            q, k, v, glow, qm, gate = _inproj_a(x2, norm_w[i], w, widths, tm)
            mix = _gla(q, k, v, glow, gate, wgu, bg, nw, B, T)
        else:
            mix_pad = None
            *qkv_groups, qm, gate = _inproj_b(x2, norm_w[i], w_in_b[j].astype(BF16), T, tm)
            mix = _dilated(qkv_groups, gate, B, T)
        x2 = _out_proj(mix.reshape(M, -1), qm, gate, kv, i, w_out[i].astype(BF16), x2, final_norm_w,
                       T, OUT_TILE, final, mix_pad)
    return x2.reshape(B, T, D)
```

```python
import functools

import jax
import jax.numpy as jnp
from jax import lax
from jax.experimental import pallas as pl
from jax.experimental.pallas import tpu as pltpu

F32 = jnp.float32
BF16 = jnp.bfloat16

LANE = 128
NORM_EPS = 1e-6
LOG2_E = 1.4426950408889634
MEM_HEADS = 4
MEM_HEAD_DIM = 64
MEM_WIDTH = MEM_HEADS * MEM_HEAD_DIM
GLA_HEADS = 4
GLA_DK = 96
GLA_DV = 192
GLA_DK_PAD = 128
GLA_DV_PAD = 256
GLA_RANK = 16
GLA_TAU = 16.0
GLA_CHUNK = 64
GLA_UNROLL = 2
GLA_HEADS_PER_STEP = 4
DIL_GROUPS = ((128, 1), (512, 4), (2048, 16))
DIL_HEADS = 6
DIL_HEAD_DIM = 128
DIL_BLOCK = 128
DIL_BATCH = 2
DIL_HEADS_PER_STEP = 3
OUT_ROWS = 256
OUT_TILE = 1024
ROPE_THETA = 500000.0
ROPE_DIM = 32
VMEM_LIMIT = 56 * 1024 * 1024

_NT = (((1,), (1,)), ((), ()))
_TN = (((0,), (0,)), ((), ()))


def _dot(a, b):
    return jnp.dot(a, b, preferred_element_type=F32)


def _dot_nt(a, b):
    return lax.dot_general(a, b, _NT, preferred_element_type=F32)


def _dot_tn(a, b):
    return lax.dot_general(a, b, _TN, preferred_element_type=F32)


def _rmsnorm(x, w):
    ms = jnp.mean(x * x, axis=-1, keepdims=True)
    return x * lax.rsqrt(ms + NORM_EPS) * w


def _silu(g):
    return g * (0.5 + 0.5 * jnp.tanh(0.5 * g))


def _software_pipeline(n_items, stages):
    carries = [None] * n_items
    for t in range(n_items + len(stages) - 1):
        for k, stage in enumerate(stages):
            i = t - k
            if 0 <= i < n_items:
                carries[i] = stage(i, carries[i])


def _params(n_parallel):
    return pltpu.CompilerParams(
        dimension_semantics=("parallel",) * n_parallel,
        vmem_limit_bytes=VMEM_LIMIT)


def _memkv_kernel(mem_ref, nw_ref, w_ref, kv_ref, *, depth):
    h = _rmsnorm(mem_ref[0], nw_ref[...]).astype(BF16)
    kv = _dot(h, w_ref[...])
    for l in range(depth):
        base = l * 2 * MEM_WIDTH
        k = kv[:, base:base + MEM_WIDTH] * (MEM_HEAD_DIM ** -0.5)
        kv_ref[l, 0, 0] = k.astype(BF16)
        kv_ref[l, 1, 0] = kv[:, base + MEM_WIDTH:base + 2 * MEM_WIDTH].astype(BF16)


def _memkv(mem, mem_norm_w, w_memkv):
    B, n_mem, D = mem.shape
    depth = w_memkv.shape[0]
    w = jnp.transpose(w_memkv, (1, 0, 2)).reshape(D, depth * 2 * MEM_WIDTH).astype(BF16)
    return pl.pallas_call(
        functools.partial(_memkv_kernel, depth=depth),
        grid=(B,),
        in_specs=[
            pl.BlockSpec((1, n_mem, D), lambda b: (b, 0, 0)),
            pl.BlockSpec((1, D), lambda b: (0, 0)),
            pl.BlockSpec((D, depth * 2 * MEM_WIDTH), lambda b: (0, 0)),
        ],
        out_specs=pl.BlockSpec((depth, 2, 1, n_mem, MEM_WIDTH), lambda b: (0, 0, b, 0, 0)),
        out_shape=jax.ShapeDtypeStruct((depth, 2, B, n_mem, MEM_WIDTH), BF16),
        compiler_params=_params(1),
        name="memkv",
    )(mem, mem_norm_w.reshape(1, D), w)


def _inproj_a_kernel(x_ref, nw_ref, w_ref, q_ref, k_ref, v_ref, gl_ref, qm_ref, gate_ref):
    h = _rmsnorm(x_ref[...], nw_ref[...]).astype(BF16)
    tm = h.shape[0]
    hk = GLA_HEADS * GLA_DK
    mixw = GLA_HEADS * GLA_DV

    def pad_heads(val, width, padded):
        zeros = jnp.zeros((tm, padded - width), val.dtype)
        pieces = []
        for hh in range(GLA_HEADS):
            pieces += [val[:, hh * width:(hh + 1) * width], zeros]
        return jnp.concatenate(pieces, axis=1).astype(BF16)

    c0 = 0
    qk = _dot(h, w_ref[:, c0:c0 + 2 * hk])
    q_ref[...] = pad_heads(qk[:, :hk], GLA_DK, GLA_DK_PAD)
    k_ref[...] = pad_heads(qk[:, hk:], GLA_DK, GLA_DK_PAD)
    c0 += 2 * hk
    v_ref[...] = pad_heads(_dot(h, w_ref[:, c0:c0 + mixw]), GLA_DV, GLA_DV_PAD)
    c0 += mixw
    gate = _dot(h, w_ref[:, c0:c0 + mixw + MEM_WIDTH])
    gate_ref[:, :GLA_HEADS * GLA_DV_PAD] = pad_heads(gate[:, :mixw], GLA_DV, GLA_DV_PAD)
    gate_ref[:, GLA_HEADS * GLA_DV_PAD:] = gate[:, mixw:].astype(BF16)
    c0 += mixw + MEM_WIDTH
    qm_ref[...] = _dot(h, w_ref[:, c0:c0 + MEM_WIDTH]).astype(BF16)
    c0 += MEM_WIDTH
    gl_ref[...] = _dot(h, w_ref[:, c0:]).astype(BF16)


def _inproj_a(x2, norm_w, w, widths, tm):
    M, D = x2.shape
    N = w.shape[1]
    return pl.pallas_call(
        _inproj_a_kernel,
        grid=(M // tm,),
        in_specs=[
            pl.BlockSpec((tm, D), lambda i: (i, 0)),
            pl.BlockSpec((1, D), lambda i: (0, 0)),
            pl.BlockSpec((D, N), lambda i: (0, 0)),
        ],
        out_specs=[pl.BlockSpec((tm, cw), lambda i: (i, 0)) for cw in widths],
        out_shape=[jax.ShapeDtypeStruct((M, cw), BF16) for cw in widths],
        compiler_params=_params(1),
        name="inproj_gla",
    )(x2, norm_w.reshape(1, D), w)


def _inproj_b_kernel(x_ref, nw_ref, w_ref, rope_ref, qkv0_ref, qkv1_ref, qkv2_ref, qm_ref, gate_ref,
                     h_scr):
    hf = _rmsnorm(x_ref[...], nw_ref[...])
    h = hf.astype(BF16)
    tm, D = hf.shape
    hd = DIL_HEAD_DIM
    hw = DIL_HEADS * hd
    n_lt = D // LANE
    for lt in range(n_lt):
        h_scr[lt] = hf[:, lt * LANE:(lt + 1) * LANE]
    group_refs = (qkv0_ref, qkv1_ref, qkv2_ref)
    for g, (_, r) in enumerate(DIL_GROUPS):
        out_ref = group_refs[g]
        rows_p = tm // r
        if r == 1:
            hg = h
        else:
            hg = jnp.concatenate(
                [jnp.concatenate([h_scr[lt, pl.ds(p, rows_p, stride=r), :] for lt in range(n_lt)],
                                 axis=1).astype(BF16) for p in range(r)], axis=0)
        cos, sin_up, sin_dn = rope_ref[3 * g], rope_ref[3 * g + 1], rope_ref[3 * g + 2]
        for s in range(3):
            ci = g * 3 + s
            acc = _dot(hg, w_ref[:, ci * hw:(ci + 1) * hw])
            for hh in range(DIL_HEADS):
                slab = acc[:, hh * hd:(hh + 1) * hd]
                if s < 2:
                    slab = (slab * cos + pltpu.roll(slab, ROPE_DIM // 2, 1) * sin_up
                            + pltpu.roll(slab, LANE - ROPE_DIM // 2, 1) * sin_dn)
                if s == 0:
                    slab = slab * (hd ** -0.5 * LOG2_E)
                slab = slab.astype(BF16)
                if r == 1:
                    out_ref[s * DIL_HEADS + hh] = slab
                else:
                    for p in range(r):
                        out_ref[s * DIL_HEADS + hh, :, p * hd:(p + 1) * hd] = (
                            slab[p * rows_p:(p + 1) * rows_p, :])
    c0 = len(DIL_GROUPS) * 3 * hw
    qm_ref[...] = _dot(h, w_ref[:, c0:c0 + MEM_WIDTH]).astype(BF16)
    c0 += MEM_WIDTH
    gate_ref[...] = _dot(h, w_ref[:, c0:]).astype(BF16)


def _rope_tables(T, tm):
    half = ROPE_DIM // 2
    inv = ROPE_THETA ** (-jnp.arange(half, dtype=F32) / half)
    ang = jnp.arange(T).astype(F32)[:, None] * inv[None, :]
    cos, sin = jnp.cos(ang), jnp.sin(ang)
    ones = jnp.ones((T, LANE - ROPE_DIM), F32)
    zeros = jnp.zeros((T, LANE - ROPE_DIM), F32)
    zh = jnp.zeros((T, half), F32)
    base = [jnp.concatenate([cos, cos, ones], axis=-1),
            jnp.concatenate([zh, sin, zeros], axis=-1),
            jnp.concatenate([-sin, zh, zeros], axis=-1)]
    tabs = []
    for _, r in DIL_GROUPS:
        for t in base:
            tabs.append(t.reshape(T // tm, tm // r, r, LANE).transpose(0, 2, 1, 3).reshape(T, LANE))
    return jnp.stack(tabs)


def _inproj_b(x2, norm_w, w, T, tm):
    M, D = x2.shape
    N = w.shape[1]
    hd = DIL_HEAD_DIM
    heads_per_group = 3 * DIL_HEADS
    n_tabs = 3 * len(DIL_GROUPS)
    gate_w = N - len(DIL_GROUPS) * heads_per_group * hd - MEM_WIDTH
    tpb = T // tm
    qkv_specs = [pl.BlockSpec((heads_per_group, tm // r, r * hd), lambda i: (0, i, 0))
                 for _, r in DIL_GROUPS]
    qkv_shapes = [jax.ShapeDtypeStruct((heads_per_group, M // r, r * hd), BF16)
                  for _, r in DIL_GROUPS]
    return pl.pallas_call(
        _inproj_b_kernel,
        grid=(M // tm,),
        in_specs=[
            pl.BlockSpec((tm, D), lambda i: (i, 0)),
            pl.BlockSpec((1, D), lambda i: (0, 0)),
            pl.BlockSpec((D, N), lambda i: (0, 0), pipeline_mode=pl.Buffered(1)),
            pl.BlockSpec((n_tabs, tm, LANE), lambda i: (0, i % tpb, 0)),
        ],
        out_specs=qkv_specs + [
            pl.BlockSpec((tm, MEM_WIDTH), lambda i: (i, 0)),
            pl.BlockSpec((tm, gate_w), lambda i: (i, 0)),
        ],
        out_shape=qkv_shapes + [
            jax.ShapeDtypeStruct((M, MEM_WIDTH), BF16),
            jax.ShapeDtypeStruct((M, gate_w), BF16),
        ],
        scratch_shapes=[pltpu.VMEM((D // LANE, tm, LANE), F32)],
        compiler_params=_params(1),
        name="inproj_dil",
    )(x2, norm_w.reshape(1, D), w, _rope_tables(T, tm))


def _gla_kernel(q_ref, k_ref, v_ref, gl_ref, gate_ref, wgu_ref, bg_ref, nw_ref, o_ref):
    T = q_ref.shape[1]
    C = GLA_CHUNK
    U = GLA_UNROLL
    R = U * C
    kw, vw = GLA_DK_PAD, GLA_DV_PAD
    heads = q_ref.shape[2] // kw
    row = lax.broadcasted_iota(jnp.int32, (R, R), 0)
    col = lax.broadcasted_iota(jnp.int32, (R, R), 1)
    shift = C.bit_length() - 1
    in_chunk_causal = ((row >> shift) == (col >> shift)) & (row >= col)
    tri = jnp.where(in_chunk_causal, 1.0, 0.0).astype(BF16)
    tri3 = jnp.concatenate([tri, tri, tri], axis=1)
    state = [jnp.zeros((vw, kw), F32) for _ in range(heads)]

    def gate_logits(i, _):
        hh, ci = i % heads, i // heads
        rows = slice(ci * R, (ci + 1) * R)
        kl = slice(hh * kw, (hh + 1) * kw)
        z = _dot(gl_ref[0, rows, :], wgu_ref[:, kl]) + bg_ref[:, kl]
        la = (jnp.minimum(z, 0.0) * (LOG2_E / GLA_TAU)
              - jnp.log2(1.0 + jnp.exp2(jnp.abs(z) * -LOG2_E)) * (1.0 / GLA_TAU))
        hi = la.astype(BF16)
        rem = la - hi.astype(F32)
        mid = rem.astype(BF16)
        lo = (rem - mid.astype(F32)).astype(BF16)
        return hh, rows, kl, jnp.concatenate([hi, mid, lo], axis=0)

    def decays(i, carry):
        hh, rows, kl, pieces = carry
        b = _dot(tri3, pieces)
        b_last = [b[c * C + C - 1:(c + 1) * C, :] for c in range(U)]
        bl_rows = jnp.concatenate([jnp.broadcast_to(bl, (C, bl.shape[-1])) for bl in b_last], axis=0)
        qc = q_ref[0, rows, kl].astype(F32) * (GLA_DK ** -0.5)
        kc = k_ref[0, rows, kl].astype(F32)
        q_in = (qc * jnp.exp2(b)).astype(BF16)
        k_in = (kc * jnp.exp2(-b)).astype(BF16)
        k_out = (kc * jnp.exp2(bl_rows - b)).astype(BF16)
        return hh, rows, q_in, k_in, k_out, [jnp.exp2(bl) for bl in b_last]

    def intra(i, carry):
        hh, rows, q_in, k_in, k_out, dec = carry
        vc = v_ref[0, rows, hh * vw:(hh + 1) * vw]
        a = _dot_nt(q_in, k_in)
        ds = [_dot_tn(vc[c * C:(c + 1) * C], k_out[c * C:(c + 1) * C]) for c in range(U)]
        o_intra = _dot(jnp.where(in_chunk_causal, a, 0.0).astype(BF16), vc)
        return hh, rows, q_in, o_intra, ds, dec

    def inter(i, carry):
        hh, rows, q_in, o_intra, ds, dec = carry
        vl = slice(hh * vw, (hh + 1) * vw)
        st = state[hh]
        o_inter = []
        for c in range(U):
            o_inter.append(_dot_nt(q_in[c * C:(c + 1) * C], st.astype(BF16)))
            st = st * dec[c] + ds[c]
        state[hh] = st
        o = o_intra + jnp.concatenate(o_inter, axis=0)
        ms = jnp.sum(o * o, axis=-1, keepdims=True) * (1.0 / GLA_DV)
        y = o * lax.rsqrt(ms + NORM_EPS) * nw_ref[...]
        g = gate_ref[0, rows, vl].astype(F32)
        o_ref[0, rows, vl] = (y * _silu(g)).astype(BF16)
        return None

    _software_pipeline(heads * (T // R), (gate_logits, decays, intra, inter))


def _gla(q, k, v, glow, gate, wgu, bg, nw, B, T):
    hps = GLA_HEADS_PER_STEP
    kw, vw = hps * GLA_DK_PAD, hps * GLA_DV_PAD
    q3 = q.reshape(B, T, q.shape[-1])
    k3 = k.reshape(B, T, k.shape[-1])
    v3 = v.reshape(B, T, v.shape[-1])
    g3 = glow.reshape(B, T, LANE)
    gate3 = gate.reshape(B, T, gate.shape[-1])
    return pl.pallas_call(
        _gla_kernel,
        grid=(B, GLA_HEADS // hps),
        in_specs=[
            pl.BlockSpec((1, T, kw), lambda b, h: (b, 0, h)),
            pl.BlockSpec((1, T, kw), lambda b, h: (b, 0, h)),
            pl.BlockSpec((1, T, vw), lambda b, h: (b, 0, h)),
            pl.BlockSpec((1, T, LANE), lambda b, h: (b, 0, 0)),
            pl.BlockSpec((1, T, vw), lambda b, h: (b, 0, h)),
            pl.BlockSpec((LANE, kw), lambda b, h: (0, h)),
            pl.BlockSpec((1, kw), lambda b, h: (0, h)),
            pl.BlockSpec((1, GLA_DV_PAD), lambda b, h: (0, 0)),
        ],
        out_specs=pl.BlockSpec((1, T, vw), lambda b, h: (b, 0, h)),
        out_shape=jax.ShapeDtypeStruct((B, T, GLA_HEADS * GLA_DV_PAD), BF16),
        compiler_params=_params(2),
        name="gla",
    )(q3, k3, v3, g3, gate3, wgu, bg, nw)


def _dil_kernel(q0, k0, v0, q1, k1, v1, q2, k2, v2, gate_ref, o_ref, o_scr, l_scr):
    T = o_ref.shape[1]
    n = DIL_BLOCK
    hd = DIL_HEAD_DIM
    heads = q0.shape[0]
    qi = lax.broadcasted_iota(jnp.int32, (n, 2 * n), 0)
    ki = lax.broadcasted_iota(jnp.int32, (n, 2 * n), 1)
    band = (ki >= qi) & (ki <= qi + n)
    causal = (lax.broadcasted_iota(jnp.int32, (n, n), 1)
              <= lax.broadcasted_iota(jnp.int32, (n, n), 0))
    groups = ((q0, k0, v0), (q1, k1, v1), (q2, k2, v2))

    dilated = [g for g, (_, r) in enumerate(DIL_GROUPS) if r > 1]
    dense = [g for g, (_, r) in enumerate(DIL_GROUPS) if r == 1]
    assert len(dense) == 1 and o_scr.shape[0] == heads * len(dilated)

    tasks = []
    for hh in range(heads):
        for g in dilated + dense:
            r = DIL_GROUPS[g][1]
            tasks += [(hh, g, p, j) for j in range(T // r // n) for p in range(r)]
    batches = [tasks[t:t + DIL_BATCH] for t in range(0, len(tasks), DIL_BATCH)]

    def scores(i, _):
        out = []
        for hh, g, p, j in batches[i]:
            qr, kr, vr = groups[g]
            cols = slice(p * hd, (p + 1) * hd)
            win = slice(0, n) if j == 0 else slice((j - 1) * n, (j + 1) * n)
            s = _dot_nt(qr[hh, 0, j * n:(j + 1) * n, cols], kr[hh, 0, win, cols])
            out.append((jnp.where(causal if j == 0 else band, s, -jnp.inf), vr[hh, 0, win, cols]))
        return out

    def row_max(i, carry):
        return [(s, vw, jnp.max(s, axis=-1, keepdims=True)) for s, vw in carry]

    def exponentials(i, carry):
        out = []
        for s, vw, m in carry:
            e = jnp.exp2(s - m)
            out.append((e.astype(BF16), vw, m, jnp.sum(e, axis=-1, keepdims=True)))
        return out

    def weighted_values(i, carry):
        for (hh, g, p, j), (e, vw, m, den) in zip(batches[i], carry):
            o = _dot(e, vw) * (1.0 / den)
            lse = jnp.broadcast_to(m + jnp.log2(den), o.shape)
            r = DIL_GROUPS[g][1]
            if r > 1:
                slot = hh * len(dilated) + dilated.index(g)
                dst = pl.ds(j * n * r + p, n, stride=r)
                o_scr[slot, dst, :] = o
                l_scr[slot, dst, :] = lse
            else:
                rows = slice(j * n, (j + 1) * n)
                slots = range(hh * len(dilated), (hh + 1) * len(dilated))
                ls = [lse] + [l_scr[slot, rows, :] for slot in slots]
                os_ = [o] + [o_scr[slot, rows, :] for slot in slots]
                mx = functools.reduce(jnp.maximum, ls)
                ws = [jnp.exp2(l - mx) for l in ls]
                num = functools.reduce(lambda a, b: a + b, [w * ov for w, ov in zip(ws, os_)])
                mixed = num / functools.reduce(lambda a, b: a + b, ws)
                cols = slice(hh * hd, (hh + 1) * hd)
                gt = gate_ref[0, rows, cols].astype(F32)
                o_ref[0, rows, cols] = (mixed * _silu(gt)).astype(BF16)
        return None

    _software_pipeline(len(batches), (scores, row_max, exponentials, weighted_values))


def _dilated(qkv_groups, gate, B, T):
    hd = DIL_HEAD_DIM
    hps = DIL_HEADS_PER_STEP
    steps = DIL_HEADS // hps
    in_specs, args = [], []
    for (_, r), qkv in zip(DIL_GROUPS, qkv_groups):
        view = qkv.reshape(qkv.shape[0], B, T // r, r * hd)
        for s in range(3):
            in_specs.append(pl.BlockSpec((hps, 1, T // r, r * hd),
                                         lambda b, h, s=s: (s * steps + h, b, 0, 0)))
            args.append(view)
    in_specs.append(pl.BlockSpec((1, T, hps * hd), lambda b, h: (b, 0, h)))
    args.append(gate.reshape(B, T, gate.shape[-1]))
    n_slots = hps * sum(r > 1 for _, r in DIL_GROUPS)
    return pl.pallas_call(
        _dil_kernel,
        grid=(B, steps),
        in_specs=in_specs,
        out_specs=pl.BlockSpec((1, T, hps * hd), lambda b, h: (b, 0, h)),
        out_shape=jax.ShapeDtypeStruct((B, T, DIL_HEADS * hd), BF16),
        scratch_shapes=[pltpu.VMEM((n_slots, T, hd), F32), pltpu.VMEM((n_slots, T, hd), F32)],
        compiler_params=_params(2),
        name="dilated",
    )(*args)


def _out_kernel(mix_ref, qm_ref, gm_ref, kst_ref, vst_ref, w_ref, x_ref, fnw_ref, o_ref, *,
                final, mix_pad):
    n_mem = kst_ref.shape[0]
    tm = mix_ref.shape[0]
    rs = OUT_ROWS
    lane = lax.broadcasted_iota(jnp.int32, (n_mem, MEM_WIDTH), 1)
    head_lanes = [(lane >= hh * MEM_HEAD_DIM) & (lane < (hh + 1) * MEM_HEAD_DIM)
                  for hh in range(MEM_HEADS)]
    zero = jnp.zeros((n_mem, MEM_WIDTH), BF16)
    k_heads = jnp.concatenate([jnp.where(msk, kst_ref[...], zero) for msk in head_lanes], axis=0)
    v_heads = [jnp.where(msk, vst_ref[...], zero) for msk in head_lanes]

    def mixer_rows(rows):
        mix = mix_ref[rows, :]
        if mix_pad is None:
            return mix
        heads, width, padded = mix_pad
        return jnp.concatenate([mix[:, hh * padded:hh * padded + width] for hh in range(heads)], axis=1)

    def scores(i, _):
        rows = slice(i * rs, (i + 1) * rs)
        return rows, _dot_nt(qm_ref[rows, :], k_heads)

    def softmax(i, carry):
        rows, s = carry
        ps = []
        for hh in range(MEM_HEADS):
            seg = s[:, hh * n_mem:(hh + 1) * n_mem]
            e = jnp.exp(seg - jnp.max(seg, axis=-1, keepdims=True))
            ps.append((e / jnp.sum(e, axis=-1, keepdims=True)).astype(BF16))
        return rows, ps

    def values(i, carry):
        rows, ps = carry
        mo = None
        for hh, p in enumerate(ps):
            part = _dot(p, v_heads[hh])
            mo = part if mo is None else mo + part
        return rows, (mo * _silu(gm_ref[rows, :].astype(F32))).astype(BF16)

    def project(i, carry):
        rows, bm = carry
        branch = jnp.concatenate([mixer_rows(rows), bm], axis=-1)
        y = _dot(branch, w_ref[...]) + x_ref[rows, :]
        if final:
            y = _rmsnorm(y, fnw_ref[...])
        o_ref[rows, :] = y
        return None

    _software_pipeline(tm // rs, (scores, softmax, values, project))


def _out_proj(mix2, qm2, gate2, kv, layer, w_out, x2, fnw, T, tm, final, mix_pad=None):
    M, D = x2.shape
    wm = mix2.shape[1]
    gate_blk = (gate2.shape[1] - MEM_WIDTH) // MEM_WIDTH
    tpb = T // tm
    kv_spec = lambda which: pl.BlockSpec(
        (None, None, None, kv.shape[3], MEM_WIDTH), lambda i: (layer, which, i // tpb, 0, 0))
    return pl.pallas_call(
        functools.partial(_out_kernel, final=final, mix_pad=mix_pad),
        grid=(M // tm,),
        in_specs=[
            pl.BlockSpec((tm, wm), lambda i: (i, 0)),
            pl.BlockSpec((tm, MEM_WIDTH), lambda i: (i, 0)),
            pl.BlockSpec((tm, MEM_WIDTH), lambda i: (i, gate_blk)),
            kv_spec(0), kv_spec(1),
            pl.BlockSpec(w_out.shape, lambda i: (0, 0)),
            pl.BlockSpec((tm, D), lambda i: (i, 0)),
            pl.BlockSpec((1, D), lambda i: (0, 0)),
        ],
        out_specs=pl.BlockSpec((tm, D), lambda i: (i, 0)),
        out_shape=jax.ShapeDtypeStruct((M, D), F32),
        compiler_params=_params(1),
        name="memattn_outproj",
    )(mix2, qm2, gate2, kv, kv, w_out, x2, fnw.reshape(1, D))


def _pad_heads(w, heads, width, padded, axis=-1):
    axis = axis % w.ndim
    shape = w.shape[:axis] + (heads, width) + w.shape[axis + 1:]
    pads = [(0, 0)] * (w.ndim + 1)
    pads[axis + 1] = (0, padded - width)
    out = jnp.pad(w.reshape(shape), pads)
    return out.reshape(w.shape[:axis] + (heads * padded,) + w.shape[axis + 1:])


def _gla_weights(w_in, w_gate_up, b_gate, gla_norm_w):
    hk = GLA_HEADS * GLA_DK
    mixw = GLA_HEADS * GLA_DV
    c = [0, hk, 2 * hk, 2 * hk + mixw, 2 * hk + mixw + GLA_RANK, 2 * hk + mixw + GLA_RANK + MEM_WIDTH]
    q, k, v = w_in[:, c[0]:c[1]], w_in[:, c[1]:c[2]], w_in[:, c[2]:c[3]]
    gl, qm, gate = w_in[:, c[3]:c[4]], w_in[:, c[4]:c[5]], w_in[:, c[5]:]
    w = jnp.concatenate([q, k, v, gate, qm, jnp.pad(gl, ((0, 0), (0, LANE - GLA_RANK)))],
                        axis=1).astype(BF16)
    widths = (GLA_HEADS * GLA_DK_PAD, GLA_HEADS * GLA_DK_PAD, GLA_HEADS * GLA_DV_PAD, LANE,
              MEM_WIDTH, GLA_HEADS * GLA_DV_PAD + MEM_WIDTH)
    wgu = jnp.pad(_pad_heads(w_gate_up, GLA_HEADS, GLA_DK, GLA_DK_PAD),
                  ((0, LANE - GLA_RANK), (0, 0))).astype(BF16)
    bg = _pad_heads(b_gate.reshape(1, hk), GLA_HEADS, GLA_DK, GLA_DK_PAD)
    nw = jnp.pad(gla_norm_w.reshape(1, GLA_DV), ((0, 0), (0, GLA_DV_PAD - GLA_DV)))
    return w, widths, wgu, bg, nw


def kernel(x, mem, mem_norm_w, norm_w, w_memkv, w_out, w_in_a, w_gate_up, b_gate, gla_norm_w,
           w_in_b, final_norm_w):
    B, T, D = x.shape
    depth = norm_w.shape[0]
    M = B * T
    tm = 512
    kv = _memkv(mem, mem_norm_w, w_memkv)
    x2 = x.reshape(M, D)
    for i in range(depth):
        j = i // 2
        final = i == depth - 1
        if i % 2 == 0:
            w, widths, wgu, bg, nw = _gla_weights(w_in_a[j], w_gate_up[j], b_gate[j], gla_norm_w[j])
            mix_pad = (GLA_HEADS, GLA_DV, GLA_DV_PAD)
            q, k, v, glow, qm, gate = _inproj_a(x2, norm_w[i], w, widths, tm)
            mix = _gla(q, k, v, glow, gate, wgu, bg, nw, B, T)
        else:
            mix_pad = None
            *qkv_groups, qm, gate = _inproj_b(x2, norm_w[i], w_in_b[j].astype(BF16), T, tm)
            mix = _dilated(qkv_groups, gate, B, T)
        x2 = _out_proj(mix.reshape(M, -1), qm, gate, kv, i, w_out[i].astype(BF16), x2, final_norm_w,
                       T, OUT_TILE, final, mix_pad)
    return x2.reshape(B, T, D)
```

```python
import functools

import jax
import jax.numpy as jnp
from jax import lax
from jax.experimental import pallas as pl
from jax.experimental.pallas import tpu as pltpu

F32 = jnp.float32
BF16 = jnp.bfloat16

LANE = 128
NORM_EPS = 1e-6
LOG2_E = 1.4426950408889634
MEM_HEADS = 4
MEM_HEAD_DIM = 64
MEM_WIDTH = MEM_HEADS * MEM_HEAD_DIM
GLA_HEADS = 4
GLA_DK = 96
GLA_DV = 192
GLA_DK_PAD = 128
GLA_DV_PAD = 256
GLA_RANK = 16
GLA_TAU = 16.0
GLA_CHUNK = 64
GLA_UNROLL = 2
GLA_HEADS_PER_STEP = 4
DIL_GROUPS = ((128, 1), (512, 4), (2048, 16))
DIL_HEADS = 6
DIL_HEAD_DIM = 128
DIL_BLOCK = 128
DIL_BATCH = 2
DIL_HEADS_PER_STEP = 3
OUT_ROWS = 256
OUT_TILE = 2048
ROPE_THETA = 500000.0
ROPE_DIM = 32
VMEM_LIMIT = 56 * 1024 * 1024

_NT = (((1,), (1,)), ((), ()))
_TN = (((0,), (0,)), ((), ()))


def _dot(a, b):
    return jnp.dot(a, b, preferred_element_type=F32)


def _dot_nt(a, b):
    return lax.dot_general(a, b, _NT, preferred_element_type=F32)


def _dot_tn(a, b):
    return lax.dot_general(a, b, _TN, preferred_element_type=F32)


def _rmsnorm(x, w):
    ms = jnp.mean(x * x, axis=-1, keepdims=True)
    return x * lax.rsqrt(ms + NORM_EPS) * w


def _silu(g):
    return g * (0.5 + 0.5 * jnp.tanh(0.5 * g))


def _software_pipeline(n_items, stages):
    carries = [None] * n_items
    for t in range(n_items + len(stages) - 1):
        for k, stage in enumerate(stages):
            i = t - k
            if 0 <= i < n_items:
                carries[i] = stage(i, carries[i])


def _params(n_parallel):
    return pltpu.CompilerParams(
        dimension_semantics=("parallel",) * n_parallel,
        vmem_limit_bytes=VMEM_LIMIT)


def _memkv_kernel(mem_ref, nw_ref, w_ref, kv_ref, *, depth):
    h = _rmsnorm(mem_ref[0], nw_ref[...]).astype(BF16)
    kv = _dot(h, w_ref[...])
    for l in range(depth):
        base = l * 2 * MEM_WIDTH
        k = kv[:, base:base + MEM_WIDTH] * (MEM_HEAD_DIM ** -0.5)
        kv_ref[l, 0, 0] = k.astype(BF16)
        kv_ref[l, 1, 0] = kv[:, base + MEM_WIDTH:base + 2 * MEM_WIDTH].astype(BF16)


def _memkv(mem, mem_norm_w, w_memkv):
    B, n_mem, D = mem.shape
    depth = w_memkv.shape[0]
    w = jnp.transpose(w_memkv, (1, 0, 2)).reshape(D, depth * 2 * MEM_WIDTH).astype(BF16)
    return pl.pallas_call(
        functools.partial(_memkv_kernel, depth=depth),
        grid=(B,),
        in_specs=[
            pl.BlockSpec((1, n_mem, D), lambda b: (b, 0, 0)),
            pl.BlockSpec((1, D), lambda b: (0, 0)),
            pl.BlockSpec((D, depth * 2 * MEM_WIDTH), lambda b: (0, 0)),
        ],
        out_specs=pl.BlockSpec((depth, 2, 1, n_mem, MEM_WIDTH), lambda b: (0, 0, b, 0, 0)),
        out_shape=jax.ShapeDtypeStruct((depth, 2, B, n_mem, MEM_WIDTH), BF16),
        compiler_params=_params(1),
        name="memkv",
    )(mem, mem_norm_w.reshape(1, D), w)


def _inproj_a_kernel(x_ref, nw_ref, w_ref, q_ref, k_ref, v_ref, gl_ref, qm_ref, gate_ref):
    h = _rmsnorm(x_ref[...], nw_ref[...]).astype(BF16)
    tm = h.shape[0]
    hk = GLA_HEADS * GLA_DK
    mixw = GLA_HEADS * GLA_DV

    def pad_heads(val, width, padded):
        zeros = jnp.zeros((tm, padded - width), val.dtype)
        pieces = []
        for hh in range(GLA_HEADS):
            pieces += [val[:, hh * width:(hh + 1) * width], zeros]
        return jnp.concatenate(pieces, axis=1).astype(BF16)

    c0 = 0
    qk = _dot(h, w_ref[:, c0:c0 + 2 * hk])
    q_ref[...] = pad_heads(qk[:, :hk], GLA_DK, GLA_DK_PAD)
    k_ref[...] = pad_heads(qk[:, hk:], GLA_DK, GLA_DK_PAD)
    c0 += 2 * hk
    v_ref[...] = pad_heads(_dot(h, w_ref[:, c0:c0 + mixw]), GLA_DV, GLA_DV_PAD)
    c0 += mixw
    gate = _dot(h, w_ref[:, c0:c0 + mixw + MEM_WIDTH])
    gate_ref[:, :GLA_HEADS * GLA_DV_PAD] = pad_heads(gate[:, :mixw], GLA_DV, GLA_DV_PAD)
    gate_ref[:, GLA_HEADS * GLA_DV_PAD:] = gate[:, mixw:].astype(BF16)
    c0 += mixw + MEM_WIDTH
    qm_ref[...] = _dot(h, w_ref[:, c0:c0 + MEM_WIDTH]).astype(BF16)
    c0 += MEM_WIDTH
    gl_ref[...] = _dot(h, w_ref[:, c0:]).astype(BF16)


def _inproj_a(x2, norm_w, w, widths, tm):
    M, D = x2.shape
    N = w.shape[1]
    return pl.pallas_call(
        _inproj_a_kernel,
        grid=(M // tm,),
        in_specs=[
            pl.BlockSpec((tm, D), lambda i: (i, 0)),
            pl.BlockSpec((1, D), lambda i: (0, 0)),
            pl.BlockSpec((D, N), lambda i: (0, 0)),
        ],
        out_specs=[pl.BlockSpec((tm, cw), lambda i: (i, 0)) for cw in widths],
        out_shape=[jax.ShapeDtypeStruct((M, cw), BF16) for cw in widths],
        compiler_params=_params(1),
        name="inproj_gla",
    )(x2, norm_w.reshape(1, D), w)


def _inproj_b_kernel(x_ref, nw_ref, w_ref, rope_ref, qkv0_ref, qkv1_ref, qkv2_ref, qm_ref, gate_ref,
                     h_scr):
    hf = _rmsnorm(x_ref[...], nw_ref[...])
    h = hf.astype(BF16)
    tm, D = hf.shape
    hd = DIL_HEAD_DIM
    hw = DIL_HEADS * hd
    n_lt = D // LANE
    for lt in range(n_lt):
        h_scr[lt] = hf[:, lt * LANE:(lt + 1) * LANE]
    group_refs = (qkv0_ref, qkv1_ref, qkv2_ref)
    for g, (_, r) in enumerate(DIL_GROUPS):
        out_ref = group_refs[g]
        rows_p = tm // r
        if r == 1:
            hg = h
        else:
            hg = jnp.concatenate(
                [jnp.concatenate([h_scr[lt, pl.ds(p, rows_p, stride=r), :] for lt in range(n_lt)],
                                 axis=1).astype(BF16) for p in range(r)], axis=0)
        cos, sin_up, sin_dn = rope_ref[3 * g], rope_ref[3 * g + 1], rope_ref[3 * g + 2]
        for s in range(3):
            ci = g * 3 + s
            acc = _dot(hg, w_ref[:, ci * hw:(ci + 1) * hw])
            for hh in range(DIL_HEADS):
                slab = acc[:, hh * hd:(hh + 1) * hd]
                if s < 2:
                    slab = (slab * cos + pltpu.roll(slab, ROPE_DIM // 2, 1) * sin_up
                            + pltpu.roll(slab, LANE - ROPE_DIM // 2, 1) * sin_dn)
                if s == 0:
                    slab = slab * (hd ** -0.5 * LOG2_E)
                slab = slab.astype(BF16)
                if r == 1:
                    out_ref[s * DIL_HEADS + hh] = slab
                else:
                    for p in range(r):
                        out_ref[s * DIL_HEADS + hh, :, p * hd:(p + 1) * hd] = (
                            slab[p * rows_p:(p + 1) * rows_p, :])
    c0 = len(DIL_GROUPS) * 3 * hw
    qm_ref[...] = _dot(h, w_ref[:, c0:c0 + MEM_WIDTH]).astype(BF16)
    c0 += MEM_WIDTH
    gate_ref[...] = _dot(h, w_ref[:, c0:]).astype(BF16)


def _rope_tables(T, tm):
    half = ROPE_DIM // 2
    inv = ROPE_THETA ** (-jnp.arange(half, dtype=F32) / half)
    ang = jnp.arange(T).astype(F32)[:, None] * inv[None, :]
    cos, sin = jnp.cos(ang), jnp.sin(ang)
    ones = jnp.ones((T, LANE - ROPE_DIM), F32)
    zeros = jnp.zeros((T, LANE - ROPE_DIM), F32)
    zh = jnp.zeros((T, half), F32)
    base = [jnp.concatenate([cos, cos, ones], axis=-1),
            jnp.concatenate([zh, sin, zeros], axis=-1),
            jnp.concatenate([-sin, zh, zeros], axis=-1)]
    tabs = []
    for _, r in DIL_GROUPS:
        for t in base:
            tabs.append(t.reshape(T // tm, tm // r, r, LANE).transpose(0, 2, 1, 3).reshape(T, LANE))
    return jnp.stack(tabs)


def _inproj_b(x2, norm_w, w, T, tm):
    M, D = x2.shape
    N = w.shape[1]
    hd = DIL_HEAD_DIM
    heads_per_group = 3 * DIL_HEADS
    n_tabs = 3 * len(DIL_GROUPS)
    gate_w = N - len(DIL_GROUPS) * heads_per_group * hd - MEM_WIDTH
    tpb = T // tm
    qkv_specs = [pl.BlockSpec((heads_per_group, tm // r, r * hd), lambda i: (0, i, 0))
                 for _, r in DIL_GROUPS]
    qkv_shapes = [jax.ShapeDtypeStruct((heads_per_group, M // r, r * hd), BF16)
                  for _, r in DIL_GROUPS]
    return pl.pallas_call(
        _inproj_b_kernel,
        grid=(M // tm,),
        in_specs=[
            pl.BlockSpec((tm, D), lambda i: (i, 0)),
            pl.BlockSpec((1, D), lambda i: (0, 0)),
            pl.BlockSpec((D, N), lambda i: (0, 0), pipeline_mode=pl.Buffered(1)),
            pl.BlockSpec((n_tabs, tm, LANE), lambda i: (0, i % tpb, 0)),
        ],
        out_specs=qkv_specs + [
            pl.BlockSpec((tm, MEM_WIDTH), lambda i: (i, 0)),
            pl.BlockSpec((tm, gate_w), lambda i: (i, 0)),
        ],
        out_shape=qkv_shapes + [
            jax.ShapeDtypeStruct((M, MEM_WIDTH), BF16),
            jax.ShapeDtypeStruct((M, gate_w), BF16),
        ],
        scratch_shapes=[pltpu.VMEM((D // LANE, tm, LANE), F32)],
        compiler_params=_params(1),
        name="inproj_dil",
    )(x2, norm_w.reshape(1, D), w, _rope_tables(T, tm))


def _gla_kernel(q_ref, k_ref, v_ref, gl_ref, gate_ref, wgu_ref, bg_ref, nw_ref, o_ref):
    T = q_ref.shape[1]
    C = GLA_CHUNK
    U = GLA_UNROLL
    R = U * C
    kw, vw = GLA_DK_PAD, GLA_DV_PAD
    heads = q_ref.shape[2] // kw
    row = lax.broadcasted_iota(jnp.int32, (R, R), 0)
    col = lax.broadcasted_iota(jnp.int32, (R, R), 1)
    shift = C.bit_length() - 1
    in_chunk_causal = ((row >> shift) == (col >> shift)) & (row >= col)
    tri = jnp.where(in_chunk_causal, 1.0, 0.0).astype(BF16)
    tri3 = jnp.concatenate([tri, tri, tri], axis=1)
    state = [jnp.zeros((vw, kw), F32) for _ in range(heads)]

    def gate_logits(i, _):
        hh, ci = i % heads, i // heads
        rows = slice(ci * R, (ci + 1) * R)
        kl = slice(hh * kw, (hh + 1) * kw)
        z = _dot(gl_ref[0, rows, :], wgu_ref[:, kl]) + bg_ref[:, kl]
        la = (jnp.minimum(z, 0.0) * (LOG2_E / GLA_TAU)
              - jnp.log2(1.0 + jnp.exp2(jnp.abs(z) * -LOG2_E)) * (1.0 / GLA_TAU))
        hi = la.astype(BF16)
        rem = la - hi.astype(F32)
        mid = rem.astype(BF16)
        lo = (rem - mid.astype(F32)).astype(BF16)
        return hh, rows, kl, jnp.concatenate([hi, mid, lo], axis=0)

    def decays(i, carry):
        hh, rows, kl, pieces = carry
        b = _dot(tri3, pieces)
        b_last = [b[c * C + C - 1:(c + 1) * C, :] for c in range(U)]
        bl_rows = jnp.concatenate([jnp.broadcast_to(bl, (C, bl.shape[-1])) for bl in b_last], axis=0)
        qc = q_ref[0, rows, kl].astype(F32) * (GLA_DK ** -0.5)
        kc = k_ref[0, rows, kl].astype(F32)
        q_in = (qc * jnp.exp2(b)).astype(BF16)
        k_in = (kc * jnp.exp2(-b)).astype(BF16)
        k_out = (kc * jnp.exp2(bl_rows - b)).astype(BF16)
        return hh, rows, q_in, k_in, k_out, [jnp.exp2(bl) for bl in b_last]

    def intra(i, carry):
        hh, rows, q_in, k_in, k_out, dec = carry
        vc = v_ref[0, rows, hh * vw:(hh + 1) * vw]
        a = _dot_nt(q_in, k_in)
        ds = [_dot_tn(vc[c * C:(c + 1) * C], k_out[c * C:(c + 1) * C]) for c in range(U)]
        o_intra = _dot(jnp.where(in_chunk_causal, a, 0.0).astype(BF16), vc)
        return hh, rows, q_in, o_intra, ds, dec

    def inter(i, carry):
        hh, rows, q_in, o_intra, ds, dec = carry
        vl = slice(hh * vw, (hh + 1) * vw)
        st = state[hh]
        o_inter = []
        for c in range(U):
            o_inter.append(_dot_nt(q_in[c * C:(c + 1) * C], st.astype(BF16)))
            st = st * dec[c] + ds[c]
        state[hh] = st
        o = o_intra + jnp.concatenate(o_inter, axis=0)
        ms = jnp.sum(o * o, axis=-1, keepdims=True) * (1.0 / GLA_DV)
        y = o * lax.rsqrt(ms + NORM_EPS) * nw_ref[...]
        g = gate_ref[0, rows, vl].astype(F32)
        o_ref[0, rows, vl] = (y * _silu(g)).astype(BF16)
        return None

    _software_pipeline(heads * (T // R), (gate_logits, decays, intra, inter))


def _gla(q, k, v, glow, gate, wgu, bg, nw, B, T):
    hps = GLA_HEADS_PER_STEP
    kw, vw = hps * GLA_DK_PAD, hps * GLA_DV_PAD
    q3 = q.reshape(B, T, q.shape[-1])
    k3 = k.reshape(B, T, k.shape[-1])
    v3 = v.reshape(B, T, v.shape[-1])
    g3 = glow.reshape(B, T, LANE)
    gate3 = gate.reshape(B, T, gate.shape[-1])
    return pl.pallas_call(
        _gla_kernel,
        grid=(B, GLA_HEADS // hps),
        in_specs=[
            pl.BlockSpec((1, T, kw), lambda b, h: (b, 0, h)),
            pl.BlockSpec((1, T, kw), lambda b, h: (b, 0, h)),
            pl.BlockSpec((1, T, vw), lambda b, h: (b, 0, h)),
            pl.BlockSpec((1, T, LANE), lambda b, h: (b, 0, 0)),
            pl.BlockSpec((1, T, vw), lambda b, h: (b, 0, h)),
            pl.BlockSpec((LANE, kw), lambda b, h: (0, h)),
            pl.BlockSpec((1, kw), lambda b, h: (0, h)),
            pl.BlockSpec((1, GLA_DV_PAD), lambda b, h: (0, 0)),
        ],
        out_specs=pl.BlockSpec((1, T, vw), lambda b, h: (b, 0, h)),
        out_shape=jax.ShapeDtypeStruct((B, T, GLA_HEADS * GLA_DV_PAD), BF16),
        compiler_params=_params(2),
        name="gla",
    )(q3, k3, v3, g3, gate3, wgu, bg, nw)


def _dil_kernel(q0, k0, v0, q1, k1, v1, q2, k2, v2, gate_ref, o_ref, o_scr, l_scr):
    T = o_ref.shape[1]
    n = DIL_BLOCK
    hd = DIL_HEAD_DIM
    heads = q0.shape[0]
    qi = lax.broadcasted_iota(jnp.int32, (n, 2 * n), 0)
    ki = lax.broadcasted_iota(jnp.int32, (n, 2 * n), 1)
    band = (ki >= qi) & (ki <= qi + n)
    causal = (lax.broadcasted_iota(jnp.int32, (n, n), 1)
              <= lax.broadcasted_iota(jnp.int32, (n, n), 0))
    groups = ((q0, k0, v0), (q1, k1, v1), (q2, k2, v2))

    dilated = [g for g, (_, r) in enumerate(DIL_GROUPS) if r > 1]
    dense = [g for g, (_, r) in enumerate(DIL_GROUPS) if r == 1]
    assert len(dense) == 1 and o_scr.shape[0] == heads * len(dilated)

    tasks = []
    for hh in range(heads):
        for g in dilated + dense:
            r = DIL_GROUPS[g][1]
            tasks += [(hh, g, p, j) for j in range(T // r // n) for p in range(r)]
    batches = [tasks[t:t + DIL_BATCH] for t in range(0, len(tasks), DIL_BATCH)]

    def scores(i, _):
        out = []
        for hh, g, p, j in batches[i]:
            qr, kr, vr = groups[g]
            cols = slice(p * hd, (p + 1) * hd)
            win = slice(0, n) if j == 0 else slice((j - 1) * n, (j + 1) * n)
            s = _dot_nt(qr[hh, 0, j * n:(j + 1) * n, cols], kr[hh, 0, win, cols])
            out.append((jnp.where(causal if j == 0 else band, s, -jnp.inf), vr[hh, 0, win, cols]))
        return out

    def row_max(i, carry):
        return [(s, vw, jnp.max(s, axis=-1, keepdims=True)) for s, vw in carry]

    def exponentials(i, carry):
        out = []
        for s, vw, m in carry:
            e = jnp.exp2(s - m)
            out.append((e.astype(BF16), vw, m, jnp.sum(e, axis=-1, keepdims=True)))
        return out

    def weighted_values(i, carry):
        for (hh, g, p, j), (e, vw, m, den) in zip(batches[i], carry):
            o = _dot(e, vw) * (1.0 / den)
            lse = jnp.broadcast_to(m + jnp.log2(den), o.shape)
            r = DIL_GROUPS[g][1]
            if r > 1:
                slot = hh * len(dilated) + dilated.index(g)
                dst = pl.ds(j * n * r + p, n, stride=r)
                o_scr[slot, dst, :] = o
                l_scr[slot, dst, :] = lse
            else:
                rows = slice(j * n, (j + 1) * n)
                slots = range(hh * len(dilated), (hh + 1) * len(dilated))
                ls = [lse] + [l_scr[slot, rows, :] for slot in slots]
                os_ = [o] + [o_scr[slot, rows, :] for slot in slots]
                mx = functools.reduce(jnp.maximum, ls)
                ws = [jnp.exp2(l - mx) for l in ls]
                num = functools.reduce(lambda a, b: a + b, [w * ov for w, ov in zip(ws, os_)])
                mixed = num / functools.reduce(lambda a, b: a + b, ws)
                cols = slice(hh * hd, (hh + 1) * hd)
                gt = gate_ref[0, rows, cols].astype(F32)
                o_ref[0, rows, cols] = (mixed * _silu(gt)).astype(BF16)
        return None

    _software_pipeline(len(batches), (scores, row_max, exponentials, weighted_values))


def _dilated(qkv_groups, gate, B, T):
    hd = DIL_HEAD_DIM
    hps = DIL_HEADS_PER_STEP
    steps = DIL_HEADS // hps
    in_specs, args = [], []
    for (_, r), qkv in zip(DIL_GROUPS, qkv_groups):
        view = qkv.reshape(qkv.shape[0], B, T // r, r * hd)
        for s in range(3):
            in_specs.append(pl.BlockSpec((hps, 1, T // r, r * hd),
                                         lambda b, h, s=s: (s * steps + h, b, 0, 0)))
            args.append(view)
    in_specs.append(pl.BlockSpec((1, T, hps * hd), lambda b, h: (b, 0, h)))
    args.append(gate.reshape(B, T, gate.shape[-1]))
    n_slots = hps * sum(r > 1 for _, r in DIL_GROUPS)
    return pl.pallas_call(
        _dil_kernel,
        grid=(B, steps),
        in_specs=in_specs,
        out_specs=pl.BlockSpec((1, T, hps * hd), lambda b, h: (b, 0, h)),
        out_shape=jax.ShapeDtypeStruct((B, T, DIL_HEADS * hd), BF16),
        scratch_shapes=[pltpu.VMEM((n_slots, T, hd), F32), pltpu.VMEM((n_slots, T, hd), F32)],
        compiler_params=_params(2),
        name="dilated",
    )(*args)


def _out_kernel(mix_ref, qm_ref, gm_ref, kst_ref, vst_ref, w_ref, x_ref, fnw_ref, o_ref, *,
                final, mix_pad):
    n_mem = kst_ref.shape[0]
    tm = mix_ref.shape[0]
    rs = OUT_ROWS
    lane = lax.broadcasted_iota(jnp.int32, (n_mem, MEM_WIDTH), 1)
    head_lanes = [(lane >= hh * MEM_HEAD_DIM) & (lane < (hh + 1) * MEM_HEAD_DIM)
                  for hh in range(MEM_HEADS)]
    zero = jnp.zeros((n_mem, MEM_WIDTH), BF16)
    k_heads = jnp.concatenate([jnp.where(msk, kst_ref[...], zero) for msk in head_lanes], axis=0)
    v_heads = [jnp.where(msk, vst_ref[...], zero) for msk in head_lanes]

    def mixer_rows(rows):
        mix = mix_ref[rows, :]
        if mix_pad is None:
            return mix
        heads, width, padded = mix_pad
        return jnp.concatenate([mix[:, hh * padded:hh * padded + width] for hh in range(heads)], axis=1)

    def scores(i, _):
        rows = slice(i * rs, (i + 1) * rs)
        return rows, _dot_nt(qm_ref[rows, :], k_heads)

    def softmax(i, carry):
        rows, s = carry
        ps = []
        for hh in range(MEM_HEADS):
            seg = s[:, hh * n_mem:(hh + 1) * n_mem]
            e = jnp.exp(seg - jnp.max(seg, axis=-1, keepdims=True))
            ps.append((e / jnp.sum(e, axis=-1, keepdims=True)).astype(BF16))
        return rows, ps

    def values(i, carry):
        rows, ps = carry
        mo = None
        for hh, p in enumerate(ps):
            part = _dot(p, v_heads[hh])
            mo = part if mo is None else mo + part
        return rows, (mo * _silu(gm_ref[rows, :].astype(F32))).astype(BF16)

    def project(i, carry):
        rows, bm = carry
        branch = jnp.concatenate([mixer_rows(rows), bm], axis=-1)
        y = _dot(branch, w_ref[...]) + x_ref[rows, :]
        if final:
            y = _rmsnorm(y, fnw_ref[...])
        o_ref[rows, :] = y
        return None

    _software_pipeline(tm // rs, (scores, softmax, values, project))


def _out_proj(mix2, qm2, gate2, kv, layer, w_out, x2, fnw, T, tm, final, mix_pad=None):
    M, D = x2.shape
    wm = mix2.shape[1]
    gate_blk = (gate2.shape[1] - MEM_WIDTH) // MEM_WIDTH
    tpb = T // tm
    kv_spec = lambda which: pl.BlockSpec(
        (None, None, None, kv.shape[3], MEM_WIDTH), lambda i: (layer, which, i // tpb, 0, 0))
    return pl.pallas_call(
        functools.partial(_out_kernel, final=final, mix_pad=mix_pad),
        grid=(M // tm,),
        in_specs=[
            pl.BlockSpec((tm, wm), lambda i: (i, 0)),
            pl.BlockSpec((tm, MEM_WIDTH), lambda i: (i, 0)),
            pl.BlockSpec((tm, MEM_WIDTH), lambda i: (i, gate_blk)),
            kv_spec(0), kv_spec(1),
            pl.BlockSpec(w_out.shape, lambda i: (0, 0)),
            pl.BlockSpec((tm, D), lambda i: (i, 0)),
            pl.BlockSpec((1, D), lambda i: (0, 0)),
        ],
        out_specs=pl.BlockSpec((tm, D), lambda i: (i, 0)),
        out_shape=jax.ShapeDtypeStruct((M, D), F32),
        compiler_params=_params(1),
        name="memattn_outproj",
    )(mix2, qm2, gate2, kv, kv, w_out, x2, fnw.reshape(1, D))


def _pad_heads(w, heads, width, padded, axis=-1):
    axis = axis % w.ndim
    shape = w.shape[:axis] + (heads, width) + w.shape[axis + 1:]
    pads = [(0, 0)] * (w.ndim + 1)
    pads[axis + 1] = (0, padded - width)
    out = jnp.pad(w.reshape(shape), pads)
    return out.reshape(w.shape[:axis] + (heads * padded,) + w.shape[axis + 1:])


def _gla_weights(w_in, w_gate_up, b_gate, gla_norm_w):
    hk = GLA_HEADS * GLA_DK
    mixw = GLA_HEADS * GLA_DV
    c = [0, hk, 2 * hk, 2 * hk + mixw, 2 * hk + mixw + GLA_RANK, 2 * hk + mixw + GLA_RANK + MEM_WIDTH]
    q, k, v = w_in[:, c[0]:c[1]], w_in[:, c[1]:c[2]], w_in[:, c[2]:c[3]]
    gl, qm, gate = w_in[:, c[3]:c[4]], w_in[:, c[4]:c[5]], w_in[:, c[5]:]
    w = jnp.concatenate([q, k, v, gate, qm, jnp.pad(gl, ((0, 0), (0, LANE - GLA_RANK)))],
                        axis=1).astype(BF16)
    widths = (GLA_HEADS * GLA_DK_PAD, GLA_HEADS * GLA_DK_PAD, GLA_HEADS * GLA_DV_PAD, LANE,
              MEM_WIDTH, GLA_HEADS * GLA_DV_PAD + MEM_WIDTH)
    wgu = jnp.pad(_pad_heads(w_gate_up, GLA_HEADS, GLA_DK, GLA_DK_PAD),
                  ((0, LANE - GLA_RANK), (0, 0))).astype(BF16)
    bg = _pad_heads(b_gate.reshape(1, hk), GLA_HEADS, GLA_DK, GLA_DK_PAD)
    nw = jnp.pad(gla_norm_w.reshape(1, GLA_DV), ((0, 0), (0, GLA_DV_PAD - GLA_DV)))
    return w, widths, wgu, bg, nw


def kernel(x, mem, mem_norm_w, norm_w, w_memkv, w_out, w_in_a, w_gate_up, b_gate, gla_norm_w,
           w_in_b, final_norm_w):
    B, T, D = x.shape
    depth = norm_w.shape[0]
    M = B * T
    tm = 512
    kv = _memkv(mem, mem_norm_w, w_memkv)
    x2 = x.reshape(M, D)
    for i in range(depth):
        j = i // 2
        final = i == depth - 1
        if i % 2 == 0:
            w, widths, wgu, bg, nw = _gla_weights(w_in_a[j], w_gate_up[j], b_gate[j], gla_norm_w[j])
            mix_pad = (GLA_HEADS, GLA_DV, GLA_DV_PAD)
            q, k, v, glow, qm, gate = _inproj_a(x2, norm_w[i], w, widths, tm)
            mix = _gla(q, k, v, glow, gate, wgu, bg, nw, B, T)
        else:
            mix_pad = None
            *qkv_groups, qm, gate = _inproj_b(x2, norm_w[i], w_in_b[j].astype(BF16), T, tm)
            mix = _dilated(qkv_groups, gate, B, T)
        x2 = _out_proj(mix.reshape(M, -1), qm, gate, kv, i, w_out[i].astype(BF16), x2, final_norm_w,
                       T, OUT_TILE, final, mix_pad)
    return x2.reshape(B, T, D)
```

```python
import functools

import jax
import jax.numpy as jnp
from jax import lax
from jax.experimental import pallas as pl
from jax.experimental.pallas import tpu as pltpu

F32 = jnp.float32
BF16 = jnp.bfloat16

LANE = 128
NORM_EPS = 1e-6
LOG2_E = 1.4426950408889634
MEM_HEADS = 4
MEM_HEAD_DIM = 64
MEM_WIDTH = MEM_HEADS * MEM_HEAD_DIM
GLA_HEADS = 4
GLA_DK = 96
GLA_DV = 192
GLA_DK_PAD = 128
GLA_DV_PAD = 256
GLA_RANK = 16
GLA_TAU = 16.0
GLA_CHUNK = 64
GLA_UNROLL = 2
GLA_HEADS_PER_STEP = 4
GLA_INPROJ_TILE = 1024
DIL_GROUPS = ((128, 1), (512, 4), (2048, 16))
DIL_HEADS = 6
DIL_HEAD_DIM = 128
DIL_BLOCK = 128
DIL_BATCH = 2
DIL_HEADS_PER_STEP = 3
OUT_ROWS = 256
OUT_TILE = 2048
ROPE_THETA = 500000.0
ROPE_DIM = 32
VMEM_LIMIT = 56 * 1024 * 1024

_NT = (((1,), (1,)), ((), ()))
_TN = (((0,), (0,)), ((), ()))


def _dot(a, b):
    return jnp.dot(a, b, preferred_element_type=F32)


def _dot_nt(a, b):
    return lax.dot_general(a, b, _NT, preferred_element_type=F32)


def _dot_tn(a, b):
    return lax.dot_general(a, b, _TN, preferred_element_type=F32)


def _rmsnorm(x, w):
    ms = jnp.mean(x * x, axis=-1, keepdims=True)
    return x * lax.rsqrt(ms + NORM_EPS) * w


def _silu(g):
    return g * (0.5 + 0.5 * jnp.tanh(0.5 * g))


def _software_pipeline(n_items, stages):
    carries = [None] * n_items
    for t in range(n_items + len(stages) - 1):
        for k, stage in enumerate(stages):
            i = t - k
            if 0 <= i < n_items:
                carries[i] = stage(i, carries[i])


def _params(n_parallel):
    return pltpu.CompilerParams(
        dimension_semantics=("parallel",) * n_parallel,
        vmem_limit_bytes=VMEM_LIMIT)


def _memkv_kernel(mem_ref, nw_ref, w_ref, kv_ref, *, depth):
    h = _rmsnorm(mem_ref[0], nw_ref[...]).astype(BF16)
    kv = _dot(h, w_ref[...])
    for l in range(depth):
        base = l * 2 * MEM_WIDTH
        k = kv[:, base:base + MEM_WIDTH] * (MEM_HEAD_DIM ** -0.5)
        kv_ref[l, 0, 0] = k.astype(BF16)
        kv_ref[l, 1, 0] = kv[:, base + MEM_WIDTH:base + 2 * MEM_WIDTH].astype(BF16)


def _memkv(mem, mem_norm_w, w_memkv):
    B, n_mem, D = mem.shape
    depth = w_memkv.shape[0]
    w = jnp.transpose(w_memkv, (1, 0, 2)).reshape(D, depth * 2 * MEM_WIDTH).astype(BF16)
    return pl.pallas_call(
        functools.partial(_memkv_kernel, depth=depth),
        grid=(B,),
        in_specs=[
            pl.BlockSpec((1, n_mem, D), lambda b: (b, 0, 0)),
            pl.BlockSpec((1, D), lambda b: (0, 0)),
            pl.BlockSpec((D, depth * 2 * MEM_WIDTH), lambda b: (0, 0)),
        ],
        out_specs=pl.BlockSpec((depth, 2, 1, n_mem, MEM_WIDTH), lambda b: (0, 0, b, 0, 0)),
        out_shape=jax.ShapeDtypeStruct((depth, 2, B, n_mem, MEM_WIDTH), BF16),
        compiler_params=_params(1),
        name="memkv",
    )(mem, mem_norm_w.reshape(1, D), w)


def _inproj_a_kernel(x_ref, nw_ref, w_ref, q_ref, k_ref, v_ref, gl_ref, qm_ref, gate_ref):
    h = _rmsnorm(x_ref[...], nw_ref[...]).astype(BF16)
    tm = h.shape[0]
    hk = GLA_HEADS * GLA_DK
    mixw = GLA_HEADS * GLA_DV

    def pad_heads(val, width, padded):
        zeros = jnp.zeros((tm, padded - width), val.dtype)
        pieces = []
        for hh in range(GLA_HEADS):
            pieces += [val[:, hh * width:(hh + 1) * width], zeros]
        return jnp.concatenate(pieces, axis=1).astype(BF16)

    c0 = 0
    qk = _dot(h, w_ref[:, c0:c0 + 2 * hk])
    q_ref[...] = pad_heads(qk[:, :hk], GLA_DK, GLA_DK_PAD)
    k_ref[...] = pad_heads(qk[:, hk:], GLA_DK, GLA_DK_PAD)
    c0 += 2 * hk
    v_ref[...] = pad_heads(_dot(h, w_ref[:, c0:c0 + mixw]), GLA_DV, GLA_DV_PAD)
    c0 += mixw
    gate = _dot(h, w_ref[:, c0:c0 + mixw + MEM_WIDTH])
    gate_ref[:, :GLA_HEADS * GLA_DV_PAD] = pad_heads(gate[:, :mixw], GLA_DV, GLA_DV_PAD)
    gate_ref[:, GLA_HEADS * GLA_DV_PAD:] = gate[:, mixw:].astype(BF16)
    c0 += mixw + MEM_WIDTH
    qm_ref[...] = _dot(h, w_ref[:, c0:c0 + MEM_WIDTH]).astype(BF16)
    c0 += MEM_WIDTH
    gl_ref[...] = _dot(h, w_ref[:, c0:]).astype(BF16)


def _inproj_a(x2, norm_w, w, widths, tm):
    M, D = x2.shape
    N = w.shape[1]
    return pl.pallas_call(
        _inproj_a_kernel,
        grid=(M // tm,),
        in_specs=[
            pl.BlockSpec((tm, D), lambda i: (i, 0)),
            pl.BlockSpec((1, D), lambda i: (0, 0)),
            pl.BlockSpec((D, N), lambda i: (0, 0)),
        ],
        out_specs=[pl.BlockSpec((tm, cw), lambda i: (i, 0)) for cw in widths],
        out_shape=[jax.ShapeDtypeStruct((M, cw), BF16) for cw in widths],
        compiler_params=_params(1),
        name="inproj_gla",
    )(x2, norm_w.reshape(1, D), w)


def _inproj_b_kernel(x_ref, nw_ref, w_ref, rope_ref, qkv0_ref, qkv1_ref, qkv2_ref, qm_ref, gate_ref,
                     h_scr):
    hf = _rmsnorm(x_ref[...], nw_ref[...])
    h = hf.astype(BF16)
    tm, D = hf.shape
    hd = DIL_HEAD_DIM
    hw = DIL_HEADS * hd
    n_lt = D // LANE
    for lt in range(n_lt):
        h_scr[lt] = hf[:, lt * LANE:(lt + 1) * LANE]
    group_refs = (qkv0_ref, qkv1_ref, qkv2_ref)
    for g, (_, r) in enumerate(DIL_GROUPS):
        out_ref = group_refs[g]
        rows_p = tm // r
        if r == 1:
            hg = h
        else:
            hg = jnp.concatenate(
                [jnp.concatenate([h_scr[lt, pl.ds(p, rows_p, stride=r), :] for lt in range(n_lt)],
                                 axis=1).astype(BF16) for p in range(r)], axis=0)
        cos, sin_up, sin_dn = rope_ref[3 * g], rope_ref[3 * g + 1], rope_ref[3 * g + 2]
        for s in range(3):
            ci = g * 3 + s
            acc = _dot(hg, w_ref[:, ci * hw:(ci + 1) * hw])
            for hh in range(DIL_HEADS):
                slab = acc[:, hh * hd:(hh + 1) * hd]
                if s < 2:
                    slab = (slab * cos + pltpu.roll(slab, ROPE_DIM // 2, 1) * sin_up
                            + pltpu.roll(slab, LANE - ROPE_DIM // 2, 1) * sin_dn)
                if s == 0:
                    slab = slab * (hd ** -0.5 * LOG2_E)
                slab = slab.astype(BF16)
                if r == 1:
                    out_ref[s * DIL_HEADS + hh] = slab
                else:
                    for p in range(r):
                        out_ref[s * DIL_HEADS + hh, :, p * hd:(p + 1) * hd] = (
                            slab[p * rows_p:(p + 1) * rows_p, :])
    c0 = len(DIL_GROUPS) * 3 * hw
    qm_ref[...] = _dot(h, w_ref[:, c0:c0 + MEM_WIDTH]).astype(BF16)
    c0 += MEM_WIDTH
    gate_ref[...] = _dot(h, w_ref[:, c0:]).astype(BF16)


def _rope_tables(T, tm):
    half = ROPE_DIM // 2
    inv = ROPE_THETA ** (-jnp.arange(half, dtype=F32) / half)
    ang = jnp.arange(T).astype(F32)[:, None] * inv[None, :]
    cos, sin = jnp.cos(ang), jnp.sin(ang)
    ones = jnp.ones((T, LANE - ROPE_DIM), F32)
    zeros = jnp.zeros((T, LANE - ROPE_DIM), F32)
    zh = jnp.zeros((T, half), F32)
    base = [jnp.concatenate([cos, cos, ones], axis=-1),
            jnp.concatenate([zh, sin, zeros], axis=-1),
            jnp.concatenate([-sin, zh, zeros], axis=-1)]
    tabs = []
    for _, r in DIL_GROUPS:
        for t in base:
            tabs.append(t.reshape(T // tm, tm // r, r, LANE).transpose(0, 2, 1, 3).reshape(T, LANE))
    return jnp.stack(tabs)


def _inproj_b(x2, norm_w, w, T, tm):
    M, D = x2.shape
    N = w.shape[1]
    hd = DIL_HEAD_DIM
    heads_per_group = 3 * DIL_HEADS
    n_tabs = 3 * len(DIL_GROUPS)
    gate_w = N - len(DIL_GROUPS) * heads_per_group * hd - MEM_WIDTH
    tpb = T // tm
    qkv_specs = [pl.BlockSpec((heads_per_group, tm // r, r * hd), lambda i: (0, i, 0))
                 for _, r in DIL_GROUPS]
    qkv_shapes = [jax.ShapeDtypeStruct((heads_per_group, M // r, r * hd), BF16)
                  for _, r in DIL_GROUPS]
    return pl.pallas_call(
        _inproj_b_kernel,
        grid=(M // tm,),
        in_specs=[
            pl.BlockSpec((tm, D), lambda i: (i, 0)),
            pl.BlockSpec((1, D), lambda i: (0, 0)),
            pl.BlockSpec((D, N), lambda i: (0, 0), pipeline_mode=pl.Buffered(1)),
            pl.BlockSpec((n_tabs, tm, LANE), lambda i: (0, i % tpb, 0)),
        ],
        out_specs=qkv_specs + [
            pl.BlockSpec((tm, MEM_WIDTH), lambda i: (i, 0)),
            pl.BlockSpec((tm, gate_w), lambda i: (i, 0)),
        ],
        out_shape=qkv_shapes + [
            jax.ShapeDtypeStruct((M, MEM_WIDTH), BF16),
            jax.ShapeDtypeStruct((M, gate_w), BF16),
        ],
        scratch_shapes=[pltpu.VMEM((D // LANE, tm, LANE), F32)],
        compiler_params=_params(1),
        name="inproj_dil",
    )(x2, norm_w.reshape(1, D), w, _rope_tables(T, tm))


def _gla_kernel(q_ref, k_ref, v_ref, gl_ref, gate_ref, wgu_ref, bg_ref, nw_ref, o_ref):
    T = q_ref.shape[1]
    C = GLA_CHUNK
    U = GLA_UNROLL
    R = U * C
    kw, vw = GLA_DK_PAD, GLA_DV_PAD
    heads = q_ref.shape[2] // kw
    row = lax.broadcasted_iota(jnp.int32, (R, R), 0)
    col = lax.broadcasted_iota(jnp.int32, (R, R), 1)
    shift = C.bit_length() - 1
    in_chunk_causal = ((row >> shift) == (col >> shift)) & (row >= col)
    tri = jnp.where(in_chunk_causal, 1.0, 0.0).astype(BF16)
    tri3 = jnp.concatenate([tri, tri, tri], axis=1)
    state = [jnp.zeros((vw, kw), F32) for _ in range(heads)]

    def gate_logits(i, _):
        hh, ci = i % heads, i // heads
        rows = slice(ci * R, (ci + 1) * R)
        kl = slice(hh * kw, (hh + 1) * kw)
        z = _dot(gl_ref[0, rows, :], wgu_ref[:, kl]) + bg_ref[:, kl]
        la = (jnp.minimum(z, 0.0) * (LOG2_E / GLA_TAU)
              - jnp.log2(1.0 + jnp.exp2(jnp.abs(z) * -LOG2_E)) * (1.0 / GLA_TAU))
        hi = la.astype(BF16)
        rem = la - hi.astype(F32)
        mid = rem.astype(BF16)
        lo = (rem - mid.astype(F32)).astype(BF16)
        return hh, rows, kl, jnp.concatenate([hi, mid, lo], axis=0)

    def decays(i, carry):
        hh, rows, kl, pieces = carry
        b = _dot(tri3, pieces)
        b_last = [b[c * C + C - 1:(c + 1) * C, :] for c in range(U)]
        bl_rows = jnp.concatenate([jnp.broadcast_to(bl, (C, bl.shape[-1])) for bl in b_last], axis=0)
        qc = q_ref[0, rows, kl].astype(F32) * (GLA_DK ** -0.5)
        kc = k_ref[0, rows, kl].astype(F32)
        q_in = (qc * jnp.exp2(b)).astype(BF16)
        k_in = (kc * jnp.exp2(-b)).astype(BF16)
        k_out = (kc * jnp.exp2(bl_rows - b)).astype(BF16)
        return hh, rows, q_in, k_in, k_out, [jnp.exp2(bl) for bl in b_last]

    def intra(i, carry):
        hh, rows, q_in, k_in, k_out, dec = carry
        vc = v_ref[0, rows, hh * vw:(hh + 1) * vw]
        a = _dot_nt(q_in, k_in)
        ds = [_dot_tn(vc[c * C:(c + 1) * C], k_out[c * C:(c + 1) * C]) for c in range(U)]
        o_intra = _dot(jnp.where(in_chunk_causal, a, 0.0).astype(BF16), vc)
        return hh, rows, q_in, o_intra, ds, dec

    def inter(i, carry):
        hh, rows, q_in, o_intra, ds, dec = carry
        vl = slice(hh * vw, (hh + 1) * vw)
        st = state[hh]
        o_inter = []
        for c in range(U):
            o_inter.append(_dot_nt(q_in[c * C:(c + 1) * C], st.astype(BF16)))
            st = st * dec[c] + ds[c]
        state[hh] = st
        o = o_intra + jnp.concatenate(o_inter, axis=0)
        ms = jnp.sum(o * o, axis=-1, keepdims=True) * (1.0 / GLA_DV)
        y = o * lax.rsqrt(ms + NORM_EPS) * nw_ref[...]
        g = gate_ref[0, rows, vl].astype(F32)
        o_ref[0, rows, vl] = (y * _silu(g)).astype(BF16)
        return None

    _software_pipeline(heads * (T // R), (gate_logits, decays, intra, inter))


def _gla(q, k, v, glow, gate, wgu, bg, nw, B, T):
    hps = GLA_HEADS_PER_STEP
    kw, vw = hps * GLA_DK_PAD, hps * GLA_DV_PAD
    q3 = q.reshape(B, T, q.shape[-1])
    k3 = k.reshape(B, T, k.shape[-1])
    v3 = v.reshape(B, T, v.shape[-1])
    g3 = glow.reshape(B, T, LANE)
    gate3 = gate.reshape(B, T, gate.shape[-1])
    return pl.pallas_call(
        _gla_kernel,
        grid=(B, GLA_HEADS // hps),
        in_specs=[
            pl.BlockSpec((1, T, kw), lambda b, h: (b, 0, h)),
            pl.BlockSpec((1, T, kw), lambda b, h: (b, 0, h)),
            pl.BlockSpec((1, T, vw), lambda b, h: (b, 0, h)),
            pl.BlockSpec((1, T, LANE), lambda b, h: (b, 0, 0)),
            pl.BlockSpec((1, T, vw), lambda b, h: (b, 0, h)),
            pl.BlockSpec((LANE, kw), lambda b, h: (0, h)),
            pl.BlockSpec((1, kw), lambda b, h: (0, h)),
            pl.BlockSpec((1, GLA_DV_PAD), lambda b, h: (0, 0)),
        ],
        out_specs=pl.BlockSpec((1, T, vw), lambda b, h: (b, 0, h)),
        out_shape=jax.ShapeDtypeStruct((B, T, GLA_HEADS * GLA_DV_PAD), BF16),
        compiler_params=_params(2),
        name="gla",
    )(q3, k3, v3, g3, gate3, wgu, bg, nw)


def _dil_kernel(q0, k0, v0, q1, k1, v1, q2, k2, v2, gate_ref, o_ref, o_scr, l_scr):
    T = o_ref.shape[1]
    n = DIL_BLOCK
    hd = DIL_HEAD_DIM
    heads = q0.shape[0]
    qi = lax.broadcasted_iota(jnp.int32, (n, 2 * n), 0)
    ki = lax.broadcasted_iota(jnp.int32, (n, 2 * n), 1)
    band = (ki >= qi) & (ki <= qi + n)
    causal = (lax.broadcasted_iota(jnp.int32, (n, n), 1)
              <= lax.broadcasted_iota(jnp.int32, (n, n), 0))
    groups = ((q0, k0, v0), (q1, k1, v1), (q2, k2, v2))

    dilated = [g for g, (_, r) in enumerate(DIL_GROUPS) if r > 1]
    dense = [g for g, (_, r) in enumerate(DIL_GROUPS) if r == 1]
    assert len(dense) == 1 and o_scr.shape[0] == heads * len(dilated)

    tasks = []
    for hh in range(heads):
        for g in dilated + dense:
            r = DIL_GROUPS[g][1]
            tasks += [(hh, g, p, j) for j in range(T // r // n) for p in range(r)]
    batches = [tasks[t:t + DIL_BATCH] for t in range(0, len(tasks), DIL_BATCH)]

    def scores(i, _):
        out = []
        for hh, g, p, j in batches[i]:
            qr, kr, vr = groups[g]
            cols = slice(p * hd, (p + 1) * hd)
            win = slice(0, n) if j == 0 else slice((j - 1) * n, (j + 1) * n)
            s = _dot_nt(qr[hh, 0, j * n:(j + 1) * n, cols], kr[hh, 0, win, cols])
            out.append((jnp.where(causal if j == 0 else band, s, -jnp.inf), vr[hh, 0, win, cols]))
        return out

    def row_max(i, carry):
        return [(s, vw, jnp.max(s, axis=-1, keepdims=True)) for s, vw in carry]

    def exponentials(i, carry):
        out = []
        for s, vw, m in carry:
            e = jnp.exp2(s - m)
            out.append((e.astype(BF16), vw, m, jnp.sum(e, axis=-1, keepdims=True)))
        return out

    def weighted_values(i, carry):
        for (hh, g, p, j), (e, vw, m, den) in zip(batches[i], carry):
            o = _dot(e, vw) * (1.0 / den)
            lse = jnp.broadcast_to(m + jnp.log2(den), o.shape)
            r = DIL_GROUPS[g][1]
            if r > 1:
                slot = hh * len(dilated) + dilated.index(g)
                dst = pl.ds(j * n * r + p, n, stride=r)
                o_scr[slot, dst, :] = o
                l_scr[slot, dst, :] = lse
            else:
                rows = slice(j * n, (j + 1) * n)
                slots = range(hh * len(dilated), (hh + 1) * len(dilated))
                ls = [lse] + [l_scr[slot, rows, :] for slot in slots]
                os_ = [o] + [o_scr[slot, rows, :] for slot in slots]
                mx = functools.reduce(jnp.maximum, ls)
                ws = [jnp.exp2(l - mx) for l in ls]
                num = functools.reduce(lambda a, b: a + b, [w * ov for w, ov in zip(ws, os_)])
                mixed = num / functools.reduce(lambda a, b: a + b, ws)
                cols = slice(hh * hd, (hh + 1) * hd)
                gt = gate_ref[0, rows, cols].astype(F32)
                o_ref[0, rows, cols] = (mixed * _silu(gt)).astype(BF16)
        return None

    _software_pipeline(len(batches), (scores, row_max, exponentials, weighted_values))


def _dilated(qkv_groups, gate, B, T):
    hd = DIL_HEAD_DIM
    hps = DIL_HEADS_PER_STEP
    steps = DIL_HEADS // hps
    in_specs, args = [], []
    for (_, r), qkv in zip(DIL_GROUPS, qkv_groups):
        view = qkv.reshape(qkv.shape[0], B, T // r, r * hd)
        for s in range(3):
            in_specs.append(pl.BlockSpec((hps, 1, T // r, r * hd),
                                         lambda b, h, s=s: (s * steps + h, b, 0, 0)))
            args.append(view)
    in_specs.append(pl.BlockSpec((1, T, hps * hd), lambda b, h: (b, 0, h)))
    args.append(gate.reshape(B, T, gate.shape[-1]))
    n_slots = hps * sum(r > 1 for _, r in DIL_GROUPS)
    return pl.pallas_call(
        _dil_kernel,
        grid=(B, steps),
        in_specs=in_specs,
        out_specs=pl.BlockSpec((1, T, hps * hd), lambda b, h: (b, 0, h)),
        out_shape=jax.ShapeDtypeStruct((B, T, DIL_HEADS * hd), BF16),
        scratch_shapes=[pltpu.VMEM((n_slots, T, hd), F32), pltpu.VMEM((n_slots, T, hd), F32)],
        compiler_params=_params(2),
        name="dilated",
    )(*args)


def _out_kernel(mix_ref, qm_ref, gm_ref, kst_ref, vst_ref, w_ref, x_ref, fnw_ref, o_ref, *,
                final, mix_pad):
    n_mem = kst_ref.shape[0]
    tm = mix_ref.shape[0]
    rs = OUT_ROWS
    lane = lax.broadcasted_iota(jnp.int32, (n_mem, MEM_WIDTH), 1)
    head_lanes = [(lane >= hh * MEM_HEAD_DIM) & (lane < (hh + 1) * MEM_HEAD_DIM)
                  for hh in range(MEM_HEADS)]
    zero = jnp.zeros((n_mem, MEM_WIDTH), BF16)
    k_heads = jnp.concatenate([jnp.where(msk, kst_ref[...], zero) for msk in head_lanes], axis=0)
    v_heads = [jnp.where(msk, vst_ref[...], zero) for msk in head_lanes]

    def mixer_rows(rows):
        mix = mix_ref[rows, :]
        if mix_pad is None:
            return mix
        heads, width, padded = mix_pad
        return jnp.concatenate([mix[:, hh * padded:hh * padded + width] for hh in range(heads)], axis=1)

    def scores(i, _):
        rows = slice(i * rs, (i + 1) * rs)
        return rows, _dot_nt(qm_ref[rows, :], k_heads)

    def softmax(i, carry):
        rows, s = carry
        ps = []
        for hh in range(MEM_HEADS):
            seg = s[:, hh * n_mem:(hh + 1) * n_mem]
            e = jnp.exp(seg - jnp.max(seg, axis=-1, keepdims=True))
            ps.append((e / jnp.sum(e, axis=-1, keepdims=True)).astype(BF16))
        return rows, ps

    def values(i, carry):
        rows, ps = carry
        mo = None
        for hh, p in enumerate(ps):
            part = _dot(p, v_heads[hh])
            mo = part if mo is None else mo + part
        return rows, (mo * _silu(gm_ref[rows, :].astype(F32))).astype(BF16)

    def project(i, carry):
        rows, bm = carry
        branch = jnp.concatenate([mixer_rows(rows), bm], axis=-1)
        y = _dot(branch, w_ref[...]) + x_ref[rows, :]
        if final:
            y = _rmsnorm(y, fnw_ref[...])
        o_ref[rows, :] = y
        return None

    _software_pipeline(tm // rs, (scores, softmax, values, project))


def _out_proj(mix2, qm2, gate2, kv, layer, w_out, x2, fnw, T, tm, final, mix_pad=None):
    M, D = x2.shape
    wm = mix2.shape[1]
    gate_blk = (gate2.shape[1] - MEM_WIDTH) // MEM_WIDTH
    tpb = T // tm
    kv_spec = lambda which: pl.BlockSpec(
        (None, None, None, kv.shape[3], MEM_WIDTH), lambda i: (layer, which, i // tpb, 0, 0))
    return pl.pallas_call(
        functools.partial(_out_kernel, final=final, mix_pad=mix_pad),
        grid=(M // tm,),
        in_specs=[
            pl.BlockSpec((tm, wm), lambda i: (i, 0)),
            pl.BlockSpec((tm, MEM_WIDTH), lambda i: (i, 0)),
            pl.BlockSpec((tm, MEM_WIDTH), lambda i: (i, gate_blk)),
            kv_spec(0), kv_spec(1),
            pl.BlockSpec(w_out.shape, lambda i: (0, 0)),
            pl.BlockSpec((tm, D), lambda i: (i, 0)),
            pl.BlockSpec((1, D), lambda i: (0, 0)),
        ],
        out_specs=pl.BlockSpec((tm, D), lambda i: (i, 0)),
        out_shape=jax.ShapeDtypeStruct((M, D), F32),
        compiler_params=_params(1),
        name="memattn_outproj",
    )(mix2, qm2, gate2, kv, kv, w_out, x2, fnw.reshape(1, D))


def _pad_heads(w, heads, width, padded, axis=-1):
    axis = axis % w.ndim
    shape = w.shape[:axis] + (heads, width) + w.shape[axis + 1:]
    pads = [(0, 0)] * (w.ndim + 1)
    pads[axis + 1] = (0, padded - width)
    out = jnp.pad(w.reshape(shape), pads)
    return out.reshape(w.shape[:axis] + (heads * padded,) + w.shape[axis + 1:])


def _gla_weights(w_in, w_gate_up, b_gate, gla_norm_w):
    hk = GLA_HEADS * GLA_DK
    mixw = GLA_HEADS * GLA_DV
    c = [0, hk, 2 * hk, 2 * hk + mixw, 2 * hk + mixw + GLA_RANK, 2 * hk + mixw + GLA_RANK + MEM_WIDTH]
    q, k, v = w_in[:, c[0]:c[1]], w_in[:, c[1]:c[2]], w_in[:, c[2]:c[3]]
    gl, qm, gate = w_in[:, c[3]:c[4]], w_in[:, c[4]:c[5]], w_in[:, c[5]:]
    w = jnp.concatenate([q, k, v, gate, qm, jnp.pad(gl, ((0, 0), (0, LANE - GLA_RANK)))],
                        axis=1).astype(BF16)
    widths = (GLA_HEADS * GLA_DK_PAD, GLA_HEADS * GLA_DK_PAD, GLA_HEADS * GLA_DV_PAD, LANE,
              MEM_WIDTH, GLA_HEADS * GLA_DV_PAD + MEM_WIDTH)
    wgu = jnp.pad(_pad_heads(w_gate_up, GLA_HEADS, GLA_DK, GLA_DK_PAD),
                  ((0, LANE - GLA_RANK), (0, 0))).astype(BF16)
    bg = _pad_heads(b_gate.reshape(1, hk), GLA_HEADS, GLA_DK, GLA_DK_PAD)
    nw = jnp.pad(gla_norm_w.reshape(1, GLA_DV), ((0, 0), (0, GLA_DV_PAD - GLA_DV)))
    return w, widths, wgu, bg, nw


def kernel(x, mem, mem_norm_w, norm_w, w_memkv, w_out, w_in_a, w_gate_up, b_gate, gla_norm_w,
           w_in_b, final_norm_w):
    B, T, D = x.shape
    depth = norm_w.shape[0]
    M = B * T
    tm = 512
    kv = _memkv(mem, mem_norm_w, w_memkv)
    x2 = x.reshape(M, D)
    for i in range(depth):
        j = i // 2
        final = i == depth - 1
        if i % 2 == 0:
            w, widths, wgu, bg, nw = _gla_weights(w_in_a[j], w_gate_up[j], b_gate[j], gla_norm_w[j])
            mix_pad = (GLA_HEADS, GLA_DV, GLA_DV_PAD)
            q, k, v, glow, qm, gate = _inproj_a(x2, norm_w[i], w, widths, GLA_INPROJ_TILE)
            mix = _gla(q, k, v, glow, gate, wgu, bg, nw, B, T)
        else:
            mix_pad = None
            *qkv_groups, qm, gate = _inproj_b(x2, norm_w[i], w_in_b[j].astype(BF16), T, tm)
            mix = _dilated(qkv_groups, gate, B, T)
        x2 = _out_proj(mix.reshape(M, -1), qm, gate, kv, i, w_out[i].astype(BF16), x2, final_norm_w,
                       T, OUT_TILE, final, mix_pad)
    return x2.reshape(B, T, D)
```

```python
import functools

import jax
import jax.numpy as jnp
from jax import lax
from jax.experimental import pallas as pl
from jax.experimental.pallas import tpu as pltpu

F32 = jnp.float32
BF16 = jnp.bfloat16

LANE = 128
NORM_EPS = 1e-6
LOG2_E = 1.4426950408889634
MEM_HEADS = 4
MEM_HEAD_DIM = 64
MEM_WIDTH = MEM_HEADS * MEM_HEAD_DIM
GLA_HEADS = 4
GLA_DK = 96
GLA_DV = 192
GLA_DK_PAD = 128
GLA_DV_PAD = 256
GLA_RANK = 16
GLA_TAU = 16.0
GLA_CHUNK = 64
GLA_UNROLL = 2
GLA_HEADS_PER_STEP = 4
GLA_INPROJ_TILE = 1024
DIL_GROUPS = ((128, 1), (512, 4), (2048, 16))
DIL_HEADS = 6
DIL_HEAD_DIM = 128
DIL_BLOCK = 128
DIL_BATCH = 2
DIL_HEADS_PER_STEP = 3
OUT_ROWS = 256
OUT_TILE = 2048
ROPE_THETA = 500000.0
ROPE_DIM = 32
VMEM_LIMIT = 56 * 1024 * 1024

_NT = (((1,), (1,)), ((), ()))
_TN = (((0,), (0,)), ((), ()))


def _dot(a, b):
    return jnp.dot(a, b, preferred_element_type=F32)


def _dot_nt(a, b):
    return lax.dot_general(a, b, _NT, preferred_element_type=F32)


def _dot_tn(a, b):
    return lax.dot_general(a, b, _TN, preferred_element_type=F32)


def _rmsnorm(x, w):
    ms = jnp.mean(x * x, axis=-1, keepdims=True)
    return x * lax.rsqrt(ms + NORM_EPS) * w


def _silu(g):
    return g * (0.5 + 0.5 * jnp.tanh(0.5 * g))


def _software_pipeline(n_items, stages):
    carries = [None] * n_items
    for t in range(n_items + len(stages) - 1):
        for k, stage in enumerate(stages):
            i = t - k
            if 0 <= i < n_items:
                carries[i] = stage(i, carries[i])


def _params(n_parallel):
    return pltpu.CompilerParams(
        dimension_semantics=("parallel",) * n_parallel,
        vmem_limit_bytes=VMEM_LIMIT)


def _memkv_kernel(mem_ref, nw_ref, w_ref, kv_ref, *, depth):
    h = _rmsnorm(mem_ref[0], nw_ref[...]).astype(BF16)
    kv = _dot(h, w_ref[...])
    for l in range(depth):
        base = l * 2 * MEM_WIDTH
        k = kv[:, base:base + MEM_WIDTH] * (MEM_HEAD_DIM ** -0.5)
        kv_ref[l, 0, 0] = k.astype(BF16)
        kv_ref[l, 1, 0] = kv[:, base + MEM_WIDTH:base + 2 * MEM_WIDTH].astype(BF16)


def _memkv(mem, mem_norm_w, w_memkv):
    B, n_mem, D = mem.shape
    depth = w_memkv.shape[0]
    w = jnp.transpose(w_memkv, (1, 0, 2)).reshape(D, depth * 2 * MEM_WIDTH).astype(BF16)
    return pl.pallas_call(
        functools.partial(_memkv_kernel, depth=depth),
        grid=(B,),
        in_specs=[
            pl.BlockSpec((1, n_mem, D), lambda b: (b, 0, 0)),
            pl.BlockSpec((1, D), lambda b: (0, 0)),
            pl.BlockSpec((D, depth * 2 * MEM_WIDTH), lambda b: (0, 0)),
        ],
        out_specs=pl.BlockSpec((depth, 2, 1, n_mem, MEM_WIDTH), lambda b: (0, 0, b, 0, 0)),
        out_shape=jax.ShapeDtypeStruct((depth, 2, B, n_mem, MEM_WIDTH), BF16),
        compiler_params=_params(1),
        name="memkv",
    )(mem, mem_norm_w.reshape(1, D), w)


def _inproj_a_kernel(x_ref, nw_ref, w_ref, q_ref, k_ref, v_ref, gl_ref, qm_ref, gate_ref):
    h = _rmsnorm(x_ref[...], nw_ref[...]).astype(BF16)
    tm = h.shape[0]
    hk = GLA_HEADS * GLA_DK
    mixw = GLA_HEADS * GLA_DV

    def pad_heads(val, width, padded):
        zeros = jnp.zeros((tm, padded - width), val.dtype)
        pieces = []
        for hh in range(GLA_HEADS):
            pieces += [val[:, hh * width:(hh + 1) * width], zeros]
        return jnp.concatenate(pieces, axis=1).astype(BF16)

    c0 = 0
    qk = _dot(h, w_ref[:, c0:c0 + 2 * hk])
    q_ref[...] = pad_heads(qk[:, :hk], GLA_DK, GLA_DK_PAD)
    k_ref[...] = pad_heads(qk[:, hk:], GLA_DK, GLA_DK_PAD)
    c0 += 2 * hk
    v_ref[...] = pad_heads(_dot(h, w_ref[:, c0:c0 + mixw]), GLA_DV, GLA_DV_PAD)
    c0 += mixw
    gate = _dot(h, w_ref[:, c0:c0 + mixw + MEM_WIDTH])
    gate_ref[:, :GLA_HEADS * GLA_DV_PAD] = pad_heads(gate[:, :mixw], GLA_DV, GLA_DV_PAD)
    gate_ref[:, GLA_HEADS * GLA_DV_PAD:] = gate[:, mixw:].astype(BF16)
    c0 += mixw + MEM_WIDTH
    qm_ref[...] = _dot(h, w_ref[:, c0:c0 + MEM_WIDTH]).astype(BF16)
    c0 += MEM_WIDTH
    gl_ref[...] = _dot(h, w_ref[:, c0:]).astype(BF16)


def _inproj_a(x2, norm_w, w, widths, tm):
    M, D = x2.shape
    N = w.shape[1]
    return pl.pallas_call(
        _inproj_a_kernel,
        grid=(M // tm,),
        in_specs=[
            pl.BlockSpec((tm, D), lambda i: (i, 0)),
            pl.BlockSpec((1, D), lambda i: (0, 0)),
            pl.BlockSpec((D, N), lambda i: (0, 0)),
        ],
        out_specs=[pl.BlockSpec((tm, cw), lambda i: (i, 0)) for cw in widths],
        out_shape=[jax.ShapeDtypeStruct((M, cw), BF16) for cw in widths],
        compiler_params=_params(1),
        name="inproj_gla",
    )(x2, norm_w.reshape(1, D), w)


def _inproj_b_kernel(x_ref, nw_ref, w_ref, rope_ref, qkv0_ref, qkv1_ref, qkv2_ref, qm_ref, gate_ref,
                     h_scr):
    hf = _rmsnorm(x_ref[...], nw_ref[...])
    h = hf.astype(BF16)
    tm, D = hf.shape
    hd = DIL_HEAD_DIM
    hw = DIL_HEADS * hd
    n_lt = D // LANE
    for lt in range(n_lt):
        h_scr[lt] = hf[:, lt * LANE:(lt + 1) * LANE]
    group_refs = (qkv0_ref, qkv1_ref, qkv2_ref)
    for g, (_, r) in enumerate(DIL_GROUPS):
        out_ref = group_refs[g]
        rows_p = tm // r
        if r == 1:
            hg = h
        else:
            hg = jnp.concatenate(
                [jnp.concatenate([h_scr[lt, pl.ds(p, rows_p, stride=r), :] for lt in range(n_lt)],
                                 axis=1).astype(BF16) for p in range(r)], axis=0)
        cos, sin_up, sin_dn = rope_ref[3 * g], rope_ref[3 * g + 1], rope_ref[3 * g + 2]
        for s in range(3):
            ci = g * 3 + s
            acc = _dot(hg, w_ref[:, ci * hw:(ci + 1) * hw])
            for hh in range(DIL_HEADS):
                slab = acc[:, hh * hd:(hh + 1) * hd]
                if s < 2:
                    slab = (slab * cos + pltpu.roll(slab, ROPE_DIM // 2, 1) * sin_up
                            + pltpu.roll(slab, LANE - ROPE_DIM // 2, 1) * sin_dn)
                if s == 0:
                    slab = slab * (hd ** -0.5 * LOG2_E)
                slab = slab.astype(BF16)
                if r == 1:
                    out_ref[s * DIL_HEADS + hh] = slab
                else:
                    for p in range(r):
                        out_ref[s * DIL_HEADS + hh, :, p * hd:(p + 1) * hd] = (
                            slab[p * rows_p:(p + 1) * rows_p, :])
    c0 = len(DIL_GROUPS) * 3 * hw
    qm_ref[...] = _dot(h, w_ref[:, c0:c0 + MEM_WIDTH]).astype(BF16)
    c0 += MEM_WIDTH
    gate_ref[...] = _dot(h, w_ref[:, c0:]).astype(BF16)


def _rope_tables(T, tm):
    half = ROPE_DIM // 2
    inv = ROPE_THETA ** (-jnp.arange(half, dtype=F32) / half)
    ang = jnp.arange(T).astype(F32)[:, None] * inv[None, :]
    cos, sin = jnp.cos(ang), jnp.sin(ang)
    ones = jnp.ones((T, LANE - ROPE_DIM), F32)
    zeros = jnp.zeros((T, LANE - ROPE_DIM), F32)
    zh = jnp.zeros((T, half), F32)
    base = [jnp.concatenate([cos, cos, ones], axis=-1),
            jnp.concatenate([zh, sin, zeros], axis=-1),
            jnp.concatenate([-sin, zh, zeros], axis=-1)]
    tabs = []
    for _, r in DIL_GROUPS:
        for t in base:
            tabs.append(t.reshape(T // tm, tm // r, r, LANE).transpose(0, 2, 1, 3).reshape(T, LANE))
    return jnp.stack(tabs)


def _inproj_b(x2, norm_w, w_layers, layer, T, tm):
    M, D = x2.shape
    N = w_layers.shape[2]
    hd = DIL_HEAD_DIM
    heads_per_group = 3 * DIL_HEADS
    n_tabs = 3 * len(DIL_GROUPS)
    gate_w = N - len(DIL_GROUPS) * heads_per_group * hd - MEM_WIDTH
    tpb = T // tm
    qkv_specs = [pl.BlockSpec((heads_per_group, tm // r, r * hd), lambda i: (0, i, 0))
                 for _, r in DIL_GROUPS]
    qkv_shapes = [jax.ShapeDtypeStruct((heads_per_group, M // r, r * hd), BF16)
                  for _, r in DIL_GROUPS]
    return pl.pallas_call(
        _inproj_b_kernel,
        grid=(M // tm,),
        in_specs=[
            pl.BlockSpec((tm, D), lambda i: (i, 0)),
            pl.BlockSpec((1, D), lambda i: (0, 0)),
            pl.BlockSpec((None, D, N), lambda i: (layer, 0, 0), pipeline_mode=pl.Buffered(1)),
            pl.BlockSpec((n_tabs, tm, LANE), lambda i: (0, i % tpb, 0)),
        ],
        out_specs=qkv_specs + [
            pl.BlockSpec((tm, MEM_WIDTH), lambda i: (i, 0)),
            pl.BlockSpec((tm, gate_w), lambda i: (i, 0)),
        ],
        out_shape=qkv_shapes + [
            jax.ShapeDtypeStruct((M, MEM_WIDTH), BF16),
            jax.ShapeDtypeStruct((M, gate_w), BF16),
        ],
        scratch_shapes=[pltpu.VMEM((D // LANE, tm, LANE), F32)],
        compiler_params=_params(1),
        name="inproj_dil",
    )(x2, norm_w.reshape(1, D), w_layers, _rope_tables(T, tm))


def _gla_kernel(q_ref, k_ref, v_ref, gl_ref, gate_ref, wgu_ref, bg_ref, nw_ref, o_ref):
    T = q_ref.shape[1]
    C = GLA_CHUNK
    U = GLA_UNROLL
    R = U * C
    kw, vw = GLA_DK_PAD, GLA_DV_PAD
    heads = q_ref.shape[2] // kw
    row = lax.broadcasted_iota(jnp.int32, (R, R), 0)
    col = lax.broadcasted_iota(jnp.int32, (R, R), 1)
    shift = C.bit_length() - 1
    in_chunk_causal = ((row >> shift) == (col >> shift)) & (row >= col)
    tri = jnp.where(in_chunk_causal, 1.0, 0.0).astype(BF16)
    tri3 = jnp.concatenate([tri, tri, tri], axis=1)
    state = [jnp.zeros((vw, kw), F32) for _ in range(heads)]

    def gate_logits(i, _):
        hh, ci = i % heads, i // heads
        rows = slice(ci * R, (ci + 1) * R)
        kl = slice(hh * kw, (hh + 1) * kw)
        z = _dot(gl_ref[0, rows, :], wgu_ref[:, kl]) + bg_ref[:, kl]
        la = (jnp.minimum(z, 0.0) * (LOG2_E / GLA_TAU)
              - jnp.log2(1.0 + jnp.exp2(jnp.abs(z) * -LOG2_E)) * (1.0 / GLA_TAU))
        hi = la.astype(BF16)
        rem = la - hi.astype(F32)
        mid = rem.astype(BF16)
        lo = (rem - mid.astype(F32)).astype(BF16)
        return hh, rows, kl, jnp.concatenate([hi, mid, lo], axis=0)

    def decays(i, carry):
        hh, rows, kl, pieces = carry
        b = _dot(tri3, pieces)
        b_last = [b[c * C + C - 1:(c + 1) * C, :] for c in range(U)]
        bl_rows = jnp.concatenate([jnp.broadcast_to(bl, (C, bl.shape[-1])) for bl in b_last], axis=0)
        qc = q_ref[0, rows, kl].astype(F32) * (GLA_DK ** -0.5)
        kc = k_ref[0, rows, kl].astype(F32)
        q_in = (qc * jnp.exp2(b)).astype(BF16)
        k_in = (kc * jnp.exp2(-b)).astype(BF16)
        k_out = (kc * jnp.exp2(bl_rows - b)).astype(BF16)
        return hh, rows, q_in, k_in, k_out, [jnp.exp2(bl) for bl in b_last]

    def intra(i, carry):
        hh, rows, q_in, k_in, k_out, dec = carry
        vc = v_ref[0, rows, hh * vw:(hh + 1) * vw]
        a = _dot_nt(q_in, k_in)
        ds = [_dot_tn(vc[c * C:(c + 1) * C], k_out[c * C:(c + 1) * C]) for c in range(U)]
        o_intra = _dot(jnp.where(in_chunk_causal, a, 0.0).astype(BF16), vc)
        return hh, rows, q_in, o_intra, ds, dec

    def inter(i, carry):
        hh, rows, q_in, o_intra, ds, dec = carry
        vl = slice(hh * vw, (hh + 1) * vw)
        st = state[hh]
        o_inter = []
        for c in range(U):
            o_inter.append(_dot_nt(q_in[c * C:(c + 1) * C], st.astype(BF16)))
            st = st * dec[c] + ds[c]
        state[hh] = st
        o = o_intra + jnp.concatenate(o_inter, axis=0)
        ms = jnp.sum(o * o, axis=-1, keepdims=True) * (1.0 / GLA_DV)
        y = o * lax.rsqrt(ms + NORM_EPS) * nw_ref[...]
        g = gate_ref[0, rows, vl].astype(F32)
        o_ref[0, rows, vl] = (y * _silu(g)).astype(BF16)
        return None

    _software_pipeline(heads * (T // R), (gate_logits, decays, intra, inter))


def _gla(q, k, v, glow, gate, wgu, bg, nw, B, T):
    hps = GLA_HEADS_PER_STEP
    kw, vw = hps * GLA_DK_PAD, hps * GLA_DV_PAD
    q3 = q.reshape(B, T, q.shape[-1])
    k3 = k.reshape(B, T, k.shape[-1])
    v3 = v.reshape(B, T, v.shape[-1])
    g3 = glow.reshape(B, T, LANE)
    gate3 = gate.reshape(B, T, gate.shape[-1])
    return pl.pallas_call(
        _gla_kernel,
        grid=(B, GLA_HEADS // hps),
        in_specs=[
            pl.BlockSpec((1, T, kw), lambda b, h: (b, 0, h)),
            pl.BlockSpec((1, T, kw), lambda b, h: (b, 0, h)),
            pl.BlockSpec((1, T, vw), lambda b, h: (b, 0, h)),
            pl.BlockSpec((1, T, LANE), lambda b, h: (b, 0, 0)),
            pl.BlockSpec((1, T, vw), lambda b, h: (b, 0, h)),
            pl.BlockSpec((LANE, kw), lambda b, h: (0, h)),
            pl.BlockSpec((1, kw), lambda b, h: (0, h)),
            pl.BlockSpec((1, GLA_DV_PAD), lambda b, h: (0, 0)),
        ],
        out_specs=pl.BlockSpec((1, T, vw), lambda b, h: (b, 0, h)),
        out_shape=jax.ShapeDtypeStruct((B, T, GLA_HEADS * GLA_DV_PAD), BF16),
        compiler_params=_params(2),
        name="gla",
    )(q3, k3, v3, g3, gate3, wgu, bg, nw)


def _dil_kernel(q0, k0, v0, q1, k1, v1, q2, k2, v2, gate_ref, o_ref, o_scr, l_scr):
    T = o_ref.shape[1]
    n = DIL_BLOCK
    hd = DIL_HEAD_DIM
    heads = q0.shape[0]
    qi = lax.broadcasted_iota(jnp.int32, (n, 2 * n), 0)
    ki = lax.broadcasted_iota(jnp.int32, (n, 2 * n), 1)
    band = (ki >= qi) & (ki <= qi + n)
    causal = (lax.broadcasted_iota(jnp.int32, (n, n), 1)
              <= lax.broadcasted_iota(jnp.int32, (n, n), 0))
    groups = ((q0, k0, v0), (q1, k1, v1), (q2, k2, v2))

    dilated = [g for g, (_, r) in enumerate(DIL_GROUPS) if r > 1]
    dense = [g for g, (_, r) in enumerate(DIL_GROUPS) if r == 1]
    assert len(dense) == 1 and o_scr.shape[0] == heads * len(dilated)

    tasks = []
    for hh in range(heads):
        for g in dilated + dense:
            r = DIL_GROUPS[g][1]
            tasks += [(hh, g, p, j) for j in range(T // r // n) for p in range(r)]
    batches = [tasks[t:t + DIL_BATCH] for t in range(0, len(tasks), DIL_BATCH)]

    def scores(i, _):
        out = []
        for hh, g, p, j in batches[i]:
            qr, kr, vr = groups[g]
            cols = slice(p * hd, (p + 1) * hd)
            win = slice(0, n) if j == 0 else slice((j - 1) * n, (j + 1) * n)
            s = _dot_nt(qr[hh, 0, j * n:(j + 1) * n, cols], kr[hh, 0, win, cols])
            out.append((jnp.where(causal if j == 0 else band, s, -jnp.inf), vr[hh, 0, win, cols]))
        return out

    def row_max(i, carry):
        return [(s, vw, jnp.max(s, axis=-1, keepdims=True)) for s, vw in carry]

    def exponentials(i, carry):
        out = []
        for s, vw, m in carry:
            e = jnp.exp2(s - m)
            out.append((e.astype(BF16), vw, m, jnp.sum(e, axis=-1, keepdims=True)))
        return out

    def weighted_values(i, carry):
        for (hh, g, p, j), (e, vw, m, den) in zip(batches[i], carry):
            o = _dot(e, vw) * (1.0 / den)
            lse = jnp.broadcast_to(m + jnp.log2(den), o.shape)
            r = DIL_GROUPS[g][1]
            if r > 1:
                slot = hh * len(dilated) + dilated.index(g)
                dst = pl.ds(j * n * r + p, n, stride=r)
                o_scr[slot, dst, :] = o
                l_scr[slot, dst, :] = lse
            else:
                rows = slice(j * n, (j + 1) * n)
                slots = range(hh * len(dilated), (hh + 1) * len(dilated))
                ls = [lse] + [l_scr[slot, rows, :] for slot in slots]
                os_ = [o] + [o_scr[slot, rows, :] for slot in slots]
                mx = functools.reduce(jnp.maximum, ls)
                ws = [jnp.exp2(l - mx) for l in ls]
                num = functools.reduce(lambda a, b: a + b, [w * ov for w, ov in zip(ws, os_)])
                mixed = num / functools.reduce(lambda a, b: a + b, ws)
                cols = slice(hh * hd, (hh + 1) * hd)
                gt = gate_ref[0, rows, cols].astype(F32)
                o_ref[0, rows, cols] = (mixed * _silu(gt)).astype(BF16)
        return None

    _software_pipeline(len(batches), (scores, row_max, exponentials, weighted_values))


def _dilated(qkv_groups, gate, B, T):
    hd = DIL_HEAD_DIM
    hps = DIL_HEADS_PER_STEP
    steps = DIL_HEADS // hps
    in_specs, args = [], []
    for (_, r), qkv in zip(DIL_GROUPS, qkv_groups):
        view = qkv.reshape(qkv.shape[0], B, T // r, r * hd)
        for s in range(3):
            in_specs.append(pl.BlockSpec((hps, 1, T // r, r * hd),
                                         lambda b, h, s=s: (s * steps + h, b, 0, 0)))
            args.append(view)
    in_specs.append(pl.BlockSpec((1, T, hps * hd), lambda b, h: (b, 0, h)))
    args.append(gate.reshape(B, T, gate.shape[-1]))
    n_slots = hps * sum(r > 1 for _, r in DIL_GROUPS)
    return pl.pallas_call(
        _dil_kernel,
        grid=(B, steps),
        in_specs=in_specs,
        out_specs=pl.BlockSpec((1, T, hps * hd), lambda b, h: (b, 0, h)),
        out_shape=jax.ShapeDtypeStruct((B, T, DIL_HEADS * hd), BF16),
        scratch_shapes=[pltpu.VMEM((n_slots, T, hd), F32), pltpu.VMEM((n_slots, T, hd), F32)],
        compiler_params=_params(2),
        name="dilated",
    )(*args)


def _out_kernel(mix_ref, qm_ref, gm_ref, kst_ref, vst_ref, w_ref, x_ref, fnw_ref, o_ref, *,
                final, mix_pad):
    n_mem = kst_ref.shape[0]
    tm = mix_ref.shape[0]
    rs = OUT_ROWS
    lane = lax.broadcasted_iota(jnp.int32, (n_mem, MEM_WIDTH), 1)
    head_lanes = [(lane >= hh * MEM_HEAD_DIM) & (lane < (hh + 1) * MEM_HEAD_DIM)
                  for hh in range(MEM_HEADS)]
    zero = jnp.zeros((n_mem, MEM_WIDTH), BF16)
    k_heads = jnp.concatenate([jnp.where(msk, kst_ref[...], zero) for msk in head_lanes], axis=0)
    v_heads = [jnp.where(msk, vst_ref[...], zero) for msk in head_lanes]

    def mixer_rows(rows):
        mix = mix_ref[rows, :]
        if mix_pad is None:
            return mix
        heads, width, padded = mix_pad
        return jnp.concatenate([mix[:, hh * padded:hh * padded + width] for hh in range(heads)], axis=1)

    def scores(i, _):
        rows = slice(i * rs, (i + 1) * rs)
        return rows, _dot_nt(qm_ref[rows, :], k_heads)

    def softmax(i, carry):
        rows, s = carry
        ps = []
        for hh in range(MEM_HEADS):
            seg = s[:, hh * n_mem:(hh + 1) * n_mem]
            e = jnp.exp(seg - jnp.max(seg, axis=-1, keepdims=True))
            ps.append((e / jnp.sum(e, axis=-1, keepdims=True)).astype(BF16))
        return rows, ps

    def values(i, carry):
        rows, ps = carry
        mo = None
        for hh, p in enumerate(ps):
            part = _dot(p, v_heads[hh])
            mo = part if mo is None else mo + part
        return rows, (mo * _silu(gm_ref[rows, :].astype(F32))).astype(BF16)

    def project(i, carry):
        rows, bm = carry
        branch = jnp.concatenate([mixer_rows(rows), bm], axis=-1)
        y = _dot(branch, w_ref[...]) + x_ref[rows, :]
        if final:
            y = _rmsnorm(y, fnw_ref[...])
        o_ref[rows, :] = y
        return None

    _software_pipeline(tm // rs, (scores, softmax, values, project))


def _out_proj(mix2, qm2, gate2, kv, layer, w_out, x2, fnw, T, tm, final, mix_pad=None):
    M, D = x2.shape
    wm = mix2.shape[1]
    gate_blk = (gate2.shape[1] - MEM_WIDTH) // MEM_WIDTH
    tpb = T // tm
    kv_spec = lambda which: pl.BlockSpec(
        (None, None, None, kv.shape[3], MEM_WIDTH), lambda i: (layer, which, i // tpb, 0, 0))
    return pl.pallas_call(
        functools.partial(_out_kernel, final=final, mix_pad=mix_pad),
        grid=(M // tm,),
        in_specs=[
            pl.BlockSpec((tm, wm), lambda i: (i, 0)),
            pl.BlockSpec((tm, MEM_WIDTH), lambda i: (i, 0)),
            pl.BlockSpec((tm, MEM_WIDTH), lambda i: (i, gate_blk)),
            kv_spec(0), kv_spec(1),
            pl.BlockSpec(w_out.shape, lambda i: (0, 0)),
            pl.BlockSpec((tm, D), lambda i: (i, 0)),
            pl.BlockSpec((1, D), lambda i: (0, 0)),
        ],
        out_specs=pl.BlockSpec((tm, D), lambda i: (i, 0)),
        out_shape=jax.ShapeDtypeStruct((M, D), F32),
        compiler_params=_params(1),
        name="memattn_outproj",
    )(mix2, qm2, gate2, kv, kv, w_out, x2, fnw.reshape(1, D))


def _pad_heads(w, heads, width, padded, axis=-1):
    axis = axis % w.ndim
    shape = w.shape[:axis] + (heads, width) + w.shape[axis + 1:]
    pads = [(0, 0)] * (w.ndim + 1)
    pads[axis + 1] = (0, padded - width)
    out = jnp.pad(w.reshape(shape), pads)
    return out.reshape(w.shape[:axis] + (heads * padded,) + w.shape[axis + 1:])


def _gla_weights(w_in, w_gate_up, b_gate, gla_norm_w):
    hk = GLA_HEADS * GLA_DK
    mixw = GLA_HEADS * GLA_DV
    c = [0, hk, 2 * hk, 2 * hk + mixw, 2 * hk + mixw + GLA_RANK, 2 * hk + mixw + GLA_RANK + MEM_WIDTH]
    q, k, v = w_in[:, c[0]:c[1]], w_in[:, c[1]:c[2]], w_in[:, c[2]:c[3]]
    gl, qm, gate = w_in[:, c[3]:c[4]], w_in[:, c[4]:c[5]], w_in[:, c[5]:]
    w = jnp.concatenate([q, k, v, gate, qm, jnp.pad(gl, ((0, 0), (0, LANE - GLA_RANK)))],
                        axis=1).astype(BF16)
    widths = (GLA_HEADS * GLA_DK_PAD, GLA_HEADS * GLA_DK_PAD, GLA_HEADS * GLA_DV_PAD, LANE,
              MEM_WIDTH, GLA_HEADS * GLA_DV_PAD + MEM_WIDTH)
    wgu = jnp.pad(_pad_heads(w_gate_up, GLA_HEADS, GLA_DK, GLA_DK_PAD),
                  ((0, LANE - GLA_RANK), (0, 0))).astype(BF16)
    bg = _pad_heads(b_gate.reshape(1, hk), GLA_HEADS, GLA_DK, GLA_DK_PAD)
    nw = jnp.pad(gla_norm_w.reshape(1, GLA_DV), ((0, 0), (0, GLA_DV_PAD - GLA_DV)))
    return w, widths, wgu, bg, nw


def kernel(x, mem, mem_norm_w, norm_w, w_memkv, w_out, w_in_a, w_gate_up, b_gate, gla_norm_w,
           w_in_b, final_norm_w):
    B, T, D = x.shape
    depth = norm_w.shape[0]
    M = B * T
    tm = 512
    kv = _memkv(mem, mem_norm_w, w_memkv)
    w_in_b_bf16 = w_in_b.astype(BF16)
    x2 = x.reshape(M, D)
    for i in range(depth):
        j = i // 2
        final = i == depth - 1
        if i % 2 == 0:
            w, widths, wgu, bg, nw = _gla_weights(w_in_a[j], w_gate_up[j], b_gate[j], gla_norm_w[j])
            mix_pad = (GLA_HEADS, GLA_DV, GLA_DV_PAD)
            q, k, v, glow, qm, gate = _inproj_a(x2, norm_w[i], w, widths, GLA_INPROJ_TILE)
            mix = _gla(q, k, v, glow, gate, wgu, bg, nw, B, T)
        else:
            mix_pad = None
            *qkv_groups, qm, gate = _inproj_b(x2, norm_w[i], w_in_b_bf16, j, T, tm)
            mix = _dilated(qkv_groups, gate, B, T)
        x2 = _out_proj(mix.reshape(M, -1), qm, gate, kv, i, w_out[i].astype(BF16), x2, final_norm_w,
                       T, OUT_TILE, final, mix_pad)
    return x2.reshape(B, T, D)
```
